```python
import math
import jax
import jax.numpy as jnp
from jax import lax
import numpy as np

D_MODEL = 2048
BATCH = 1
SEQ = 8192
DEPTH = 2
DEC_BATCH = 128
DEC_SEQ = 1
PAST_LEN = 2048
PAGE_SIZE = 128

HEAD_DIM = 64
H_RWKV = D_MODEL // 128
H_SB = D_MODEL // 128
H_SSM = D_MODEL // 128
H_FOX = D_MODEL // 128
RW_W = H_RWKV * HEAD_DIM
SB_W = H_SB * HEAD_DIM
SSM_W = H_SSM * HEAD_DIM
FOX_W = H_FOX * HEAD_DIM
LORA_DECAY = 96
LORA_A = 96
LORA_GATE = 256
RW_PROJ = 3 * RW_W + LORA_DECAY + LORA_A + LORA_GATE
SB_PROJ = 3 * SB_W
IN_EVEN = RW_PROJ + SB_PROJ
SSM_STATE = 128
SSM_GROUPS = 4
CONV_W = 4
CONV_DIM = SSM_W + 2 * SSM_GROUPS * SSM_STATE
SSM_PROJ = SSM_W + CONV_DIM + H_SSM
FOX_PROJ = 3 * FOX_W + H_FOX
IN_ODD = SSM_PROJ + FOX_PROJ
MIX_W = RW_W + SB_W
D_FF = ((8 * D_MODEL + 3 * 256 - 1) // (3 * 256)) * 256
N_EVEN = (DEPTH + 1) // 2
N_ODD = DEPTH // 2
Q_BLOCK = 128
SSD_CHUNK = 128
NORM_EPS = 1e-6
GN_EPS = 64e-5

kernel_name = 'hybrid_rwkv7_stickbreak_mamba2_fox_step'

F32 = jnp.float32


def rmsnorm(x, g):
    x32 = x.astype(F32)
    y = x32 * lax.rsqrt(jnp.mean(x32 * x32, axis=-1, keepdims=True) + NORM_EPS)
    return (y * g).astype(x.dtype)


def modulate(x, g, shift, scale):
    x32 = x.astype(F32)
    y = x32 * lax.rsqrt(jnp.mean(x32 * x32, axis=-1, keepdims=True) + NORM_EPS)
    return (y * g * (1.0 + scale[:, None, :]) + shift[:, None, :]).astype(x.dtype)


def split_heads(u, n_heads):
    return u.reshape(u.shape[:-1] + (n_heads, HEAD_DIM))


def gather_pages(cache, layer, page_table):
    g = cache[layer, page_table]
    return g.reshape((g.shape[0], g.shape[1] * g.shape[2]) + g.shape[3:])


def query_blocks(fn, *q_arrays):
    bsz, t = q_arrays[0].shape[:2]
    nb = t // Q_BLOCK
    blocks = tuple(jnp.moveaxis(u.reshape((bsz, nb, Q_BLOCK) + u.shape[2:]), 1, 0) for u in q_arrays)
    pos = jnp.arange(t, dtype=jnp.int32).reshape(nb, Q_BLOCK)
    out = lax.map(lambda args: fn(*args), blocks + (pos,))
    return jnp.moveaxis(out, 0, 1).reshape((bsz, t) + out.shape[3:])


def stick_breaking_core(q, k, v, q_pos, k_pos):
    z = jnp.einsum('bqhd,bkhd->bhqk', q.astype(F32), k.astype(F32)) * (HEAD_DIM ** -0.5)
    causal = k_pos[None, :] < q_pos[:, None]
    log_keep = jnp.where(causal, jax.nn.log_sigmoid(-z), 0.0)
    log_later = lax.cumsum(log_keep, axis=3, reverse=True) - log_keep
    weight = jnp.where(causal, jnp.exp(jax.nn.log_sigmoid(z) + log_later), 0.0)
    return jnp.einsum('bhqk,bkhd->bqhd', weight, v.astype(F32)).astype(v.dtype)


def forgetting_core(q, k, v, fq, fk, q_pos, k_pos):
    logits = jnp.einsum('bqhd,bkhd->bhqk', q.astype(F32), k.astype(F32)) * (HEAD_DIM ** -0.5)
    logits = logits + jnp.moveaxis(fq, 1, 2)[..., :, None] - jnp.moveaxis(fk, 1, 2)[..., None, :]
    causal = k_pos[None, :] <= q_pos[:, None]
    probs = jax.nn.softmax(jnp.where(causal, logits, -jnp.inf), axis=-1)
    return jnp.einsum('bhqk,bkhd->bqhd', probs, v.astype(F32)).astype(v.dtype)


def rwkv7_mix(p, shift_prev, s0, prm):
    bsz, t = p.shape[:2]
    p_prev = jnp.concatenate([shift_prev[:, None, :].astype(p.dtype), p[:, :-1]], axis=1)
    xs = p + (p_prev - p) * prm['rw_mu']
    r, k, v, wd, ad, gd = jnp.split(
        xs, [RW_W, 2 * RW_W, 3 * RW_W, 3 * RW_W + LORA_DECAY, 3 * RW_W + LORA_DECAY + LORA_A], axis=-1)
    w_log = -jax.nn.softplus(-(prm['rw_w0'] + jnp.tanh(wd) @ prm['rw_w2']).astype(F32)) - 0.5
    decay = jnp.exp(-jnp.exp(w_log))
    a = jax.nn.sigmoid((prm['rw_a0'] + ad @ prm['rw_a2']).astype(F32))
    g = (jax.nn.sigmoid(gd) @ prm['rw_g2']).astype(F32)
    kk = split_heads((k * prm['rw_kk']).astype(F32), H_RWKV)
    kk = kk * lax.rsqrt(jnp.maximum(jnp.sum(kk * kk, axis=-1, keepdims=True), 1e-24))
    k_mod = k.astype(F32) * (1.0 + (a - 1.0) * prm['rw_ka'])
    rh, kh, vh = (split_heads(u.astype(F32), H_RWKV) for u in (r, k_mod, v))
    dh, ah = split_heads(decay, H_RWKV), split_heads(a, H_RWKV)

    def step(S, inp):
        r_t, k_t, v_t, w_t, kk_t, a_t = inp
        S = (S * w_t[:, :, None, :]
             - jnp.einsum('bhij,bhj->bhi', S, kk_t)[..., None] * (kk_t * a_t)[:, :, None, :]
             + v_t[..., None] * k_t[:, :, None, :])
        return S, jnp.einsum('bhij,bhj->bhi', S, r_t)

    seq = tuple(jnp.moveaxis(u, 1, 0) for u in (rh, kh, vh, dh, kk, ah))
    s_fin, o = lax.scan(step, s0.astype(F32), seq)
    o = jnp.moveaxis(o, 0, 1)
    mean = jnp.mean(o, axis=-1, keepdims=True)
    var = jnp.mean(jnp.square(o - mean), axis=-1, keepdims=True)
    o = ((o - mean) * lax.rsqrt(var + GN_EPS)).reshape(bsz, t, RW_W) * prm['rw_ln_g'] + prm['rw_ln_b']
    bonus = (jnp.sum(rh * kh * prm['rw_rk'], axis=-1, keepdims=True) * vh).reshape(bsz, t, RW_W)
    out = ((o + bonus) * g).astype(p.dtype)
    return out, s_fin.astype(s0.dtype), p[:, -1]


def ssd_scan(x, dt, a_head, bm, cm, h0, chunk):
    bsz, t, nh, pdim = x.shape
    nc = t // chunk
    rep = nh // SSM_GROUPS
    bh = jnp.repeat(bm, rep, axis=2)
    ch = jnp.repeat(cm, rep, axis=2)
    a = dt * a_head
    xdt = x * dt[..., None]
    cs = lambda u: u.reshape((bsz, nc, chunk) + u.shape[2:])
    a_c, x_c, b_c, c_c = cs(a), cs(xdt), cs(bh), cs(ch)
    a_cum = jnp.cumsum(a_c, axis=2)
    a_cum_h = jnp.moveaxis(a_cum, 3, 2)
    seg = a_cum_h[..., :, None] - a_cum_h[..., None, :]
    causal = jnp.tril(jnp.ones((chunk, chunk), dtype=bool))
    decay_ls = jnp.where(causal, jnp.exp(jnp.where(causal, seg, 0.0)), 0.0)
    scores = jnp.einsum('bclhn,bcshn->bchls', c_c, b_c) * decay_ls
    y_diag = jnp.einsum('bchls,bcshp->bclhp', scores, x_c)
    to_end = jnp.exp(a_cum[:, :, -1:, :] - a_cum)
    s_chunk = jnp.einsum('bclh,bclhn,bclhp->bchpn', to_end, b_c, x_c)
    chunk_decay = jnp.exp(a_cum[:, :, -1, :])

    def step(hs, inp):
        dc, sc = inp
        return hs * dc[:, :, None, None] + sc, hs

    h_fin, h_start = lax.scan(step, h0, (jnp.moveaxis(chunk_decay, 1, 0), jnp.moveaxis(s_chunk, 1, 0)))
    h_start = jnp.moveaxis(h_start, 0, 1)
    y_off = jnp.einsum('bclhn,bchpn->bclhp', c_c, h_start) * jnp.exp(a_cum)[..., None]
    return (y_diag + y_off).reshape(bsz, t, nh, pdim), h_fin


def mamba2_mix(p, conv_prev, h0, prm):
    bsz, t = p.shape[:2]
    z, xbc, dt_raw = jnp.split(p, [SSM_W, SSM_W + CONV_DIM], axis=-1)
    cin = jnp.concatenate([conv_prev.astype(p.dtype), xbc], axis=1)
    w = prm['m_conv_w']
    conv = prm['m_conv_b'] + cin[:, :t] * w[0]
    for j in range(1, CONV_W):
        conv = conv + cin[:, j:j + t] * w[j]
    xbc_act = jax.nn.silu(conv.astype(F32))
    xs, bm, cm = jnp.split(xbc_act, [SSM_W, SSM_W + SSM_GROUPS * SSM_STATE], axis=-1)
    xs = xs.reshape(bsz, t, H_SSM, HEAD_DIM)
    bm = bm.reshape(bsz, t, SSM_GROUPS, SSM_STATE)
    cm = cm.reshape(bsz, t, SSM_GROUPS, SSM_STATE)
    dt = jax.nn.softplus(dt_raw.astype(F32) + prm['m_dt_bias'])
    a_head = -jnp.exp(prm['m_a_log'].astype(F32))
    chunk = SSD_CHUNK if t % SSD_CHUNK == 0 else t
    y, h_fin = ssd_scan(xs, dt, a_head, bm, cm, h0.astype(F32), chunk)
    y = y + prm['m_d'][:, None] * xs
    y = y.reshape(bsz, t, SSM_W) * jax.nn.silu(z.astype(F32))
    yg = y.reshape(bsz, t, SSM_GROUPS, SSM_W // SSM_GROUPS)
    yg = yg * lax.rsqrt(jnp.mean(yg * yg, axis=-1, keepdims=True) + NORM_EPS)
    out = (yg.reshape(bsz, t, SSM_W) * prm['m_norm_g']).astype(p.dtype)
    return out, h_fin.astype(h0.dtype), cin[:, -(CONV_W - 1):]


def even_mixer(h, prm, caches, layer, page_table):
    bsz, t = h.shape[:2]
    p = h @ prm['w_in_even']
    p_rw, p_sb = p[..., :RW_PROJ], p[..., RW_PROJ:]
    if caches is None:
        shift0 = jnp.zeros((bsz, RW_PROJ), h.dtype)
        s0 = jnp.zeros((bsz, H_RWKV, HEAD_DIM, HEAD_DIM), h.dtype)
    else:
        shift0 = caches['state_rwkv_shift'][layer]
        s0 = caches['state_rwkv'][layer]
    o_rw, s_new, shift_new = rwkv7_mix(p_rw, shift0, s0, prm)
    q, k, v = (split_heads(u, H_SB) for u in jnp.split(p_sb, 3, axis=-1))
    q = rmsnorm(q, prm['sb_qn'])
    k = rmsnorm(k, prm['sb_kn'])
    if caches is None:
        pos = jnp.arange(t, dtype=jnp.int32)
        o_sb = query_blocks(lambda qb, pb: stick_breaking_core(qb, k, v, pb, pos), q)
    else:
        past = page_table.shape[1] * PAGE_SIZE
        k_all = jnp.concatenate([gather_pages(caches['cache_sb_k'], layer, page_table).astype(k.dtype), k], axis=1)
        v_all = jnp.concatenate([gather_pages(caches['cache_sb_v'], layer, page_table).astype(v.dtype), v], axis=1)
        k_pos = jnp.arange(past + t, dtype=jnp.int32)
        o_sb = stick_breaking_core(q, k_all, v_all, k_pos[past:], k_pos)
    mix = jnp.concatenate([o_rw, o_sb.reshape(bsz, t, SB_W)], axis=-1)
    new = {'sb_k': k, 'sb_v': v, 'rwkv': s_new, 'rwkv_shift': shift_new}
    return mix, new


def odd_mixer(h, prm, caches, layer, page_table):
    bsz, t = h.shape[:2]
    p = h @ prm['w_in_odd']
    p_ssm, p_fox = p[..., :SSM_PROJ], p[..., SSM_PROJ:]
    if caches is None:
        conv0 = jnp.zeros((bsz, CONV_W - 1, CONV_DIM), h.dtype)
        h0 = jnp.zeros((bsz, H_SSM, HEAD_DIM, SSM_STATE), h.dtype)
    else:
        conv0 = caches['state_conv'][layer]
        h0 = caches['state_ssm'][layer]
    o_ssm, h_new, conv_new = mamba2_mix(p_ssm, conv0, h0, prm)
    q, k, v, f = jnp.split(p_fox, [FOX_W, 2 * FOX_W, 3 * FOX_W], axis=-1)
    q = rmsnorm(split_heads(q, H_FOX), prm['fox_qn'])
    k = rmsnorm(split_heads(k, H_FOX), prm['fox_kn'])
    v = split_heads(v, H_FOX)
    logf = jax.nn.log_sigmoid(f.astype(F32) + prm['fox_bf'])
    if caches is None:
        pos = jnp.arange(t, dtype=jnp.int32)
        fcum = jnp.cumsum(logf, axis=1)
        o_fox = query_blocks(lambda qb, fb, pb: forgetting_core(qb, k, v, fb, fcum, pb, pos), q, fcum)
    else:
        past = page_table.shape[1] * PAGE_SIZE
        k_all = jnp.concatenate([gather_pages(caches['cache_fox_k'], layer, page_table).astype(k.dtype), k], axis=1)
        v_all = jnp.concatenate([gather_pages(caches['cache_fox_v'], layer, page_table).astype(v.dtype), v], axis=1)
        logf_all = jnp.concatenate([gather_pages(caches['cache_fox_logf'], layer, page_table).astype(F32), logf], axis=1)
        fcum = jnp.cumsum(logf_all, axis=1)
        k_pos = jnp.arange(past + t, dtype=jnp.int32)
        o_fox = forgetting_core(q, k_all, v_all, fcum[:, past:], fcum, k_pos[past:], k_pos)
    mix = jnp.concatenate([o_ssm, o_fox.reshape(bsz, t, FOX_W)], axis=-1)
    new = {'fox_k': k, 'fox_v': v, 'fox_logf': logf.astype(h.dtype), 'ssm': h_new, 'conv': conv_new}
    return mix, new


def trunk(x, c, layer_w, even_w, odd_w, caches, page_table):
    new_even, new_odd = [], []
    for l in range(DEPTH):
        lw = {name: u[l] for name, u in layer_w.items()}
        mod = jax.nn.silu(c) @ lw['w_ada'] + lw['b_ada']
        sh1, sc1, gt1, sh2, sc2, gt2 = jnp.split(mod, 6, axis=-1)
        h = modulate(x, lw['g_mix'], sh1, sc1)
        i = l // 2
        if l % 2 == 0:
            mix, new = even_mixer(h, {n: u[i] for n, u in even_w.items()}, caches, i, page_table)
            new_even.append(new)
        else:
            mix, new = odd_mixer(h, {n: u[i] for n, u in odd_w.items()}, caches, i, page_table)
            new_odd.append(new)
        x = x + gt1[:, None, :] * (mix @ lw['w_out'])
        h = modulate(x, lw['g_ffn'], sh2, sc2)
        ffn = (jax.nn.silu(h @ lw['w_ffn_gate']) * (h @ lw['w_ffn_up'])) @ lw['w_ffn_down']
        x = x + gt2[:, None, :] * ffn
    ne = {n: jnp.stack([d[n] for d in new_even], axis=0) for n in new_even[0]}
    no = {n: jnp.stack([d[n] for d in new_odd], axis=0) for n in new_odd[0]}
    return x, ne, no


def setup_inputs(seed: int = 0) -> dict:
    key = jax.random.key(seed)
    ks = iter(jax.random.split(key, 80))

    def nrm(shape, scale=1.0):
        return scale * jax.random.normal(next(ks), shape, F32)

    def unif(shape, lo, hi):
        return jax.random.uniform(next(ks), shape, F32, lo, hi)

    n_pages = PAST_LEN // PAGE_SIZE
    n_used = DEC_BATCH * n_pages
    n_pool = n_used + max(1, n_used // 4)
    page_table = jax.random.permutation(next(ks), n_pool)[:n_used].reshape(DEC_BATCH, n_pages).astype(jnp.int32)
    dt0 = jnp.exp(unif((N_ODD, H_SSM), math.log(1e-3), math.log(1e-1)))
    return {
        'x_prompt': nrm((BATCH, SEQ, D_MODEL)),
        'x_sample': nrm((DEC_BATCH, DEC_SEQ, D_MODEL)),
        'cache_sb_k': nrm((N_EVEN, n_pool, PAGE_SIZE, H_SB, HEAD_DIM)),
        'cache_sb_v': nrm((N_EVEN, n_pool, PAGE_SIZE, H_SB, HEAD_DIM)),
        'state_rwkv': nrm((N_EVEN, DEC_BATCH, H_RWKV, HEAD_DIM, HEAD_DIM)),
        'state_rwkv_shift': nrm((N_EVEN, DEC_BATCH, RW_PROJ)),
        'cache_fox_k': nrm((N_ODD, n_pool, PAGE_SIZE, H_FOX, HEAD_DIM)),
        'cache_fox_v': nrm((N_ODD, n_pool, PAGE_SIZE, H_FOX, HEAD_DIM)),
        'cache_fox_logf': jax.nn.log_sigmoid(nrm((N_ODD, n_pool, PAGE_SIZE, H_FOX)) + 2.0),
        'state_ssm': nrm((N_ODD, DEC_BATCH, H_SSM, HEAD_DIM, SSM_STATE), 0.5),
        'state_conv': nrm((N_ODD, DEC_BATCH, CONV_W - 1, CONV_DIM)),
        'page_table': page_table,
        'c_prompt': nrm((BATCH, D_MODEL)),
        'c_sample': nrm((DEC_BATCH, D_MODEL)),
        'w_ada': nrm((DEPTH, D_MODEL, 6 * D_MODEL), 0.5 * D_MODEL ** -0.5),
        'b_ada': nrm((DEPTH, 6 * D_MODEL), 0.02),
        'g_mix': 1.0 + nrm((DEPTH, D_MODEL), 0.02),
        'g_ffn': 1.0 + nrm((DEPTH, D_MODEL), 0.02),
        'w_out': nrm((DEPTH, MIX_W, D_MODEL), MIX_W ** -0.5),
        'w_ffn_gate': nrm((DEPTH, D_MODEL, D_FF), D_MODEL ** -0.5),
        'w_ffn_up': nrm((DEPTH, D_MODEL, D_FF), D_MODEL ** -0.5),
        'w_ffn_down': nrm((DEPTH, D_FF, D_MODEL), D_FF ** -0.5),
        'w_in_even': nrm((N_EVEN, D_MODEL, IN_EVEN), D_MODEL ** -0.5),
        'rw_mu': unif((N_EVEN, RW_PROJ), 0.0, 1.0),
        'rw_w0': unif((N_EVEN, RW_W), -6.0, 1.0),
        'rw_w2': nrm((N_EVEN, LORA_DECAY, RW_W), 0.5 * LORA_DECAY ** -0.5),
        'rw_a0': nrm((N_EVEN, RW_W), 0.1),
        'rw_a2': nrm((N_EVEN, LORA_A, RW_W), 0.5 * LORA_A ** -0.5),
        'rw_g2': nrm((N_EVEN, LORA_GATE, RW_W), LORA_GATE ** -0.5),
        'rw_kk': 0.85 + nrm((N_EVEN, RW_W), 0.05),
        'rw_ka': 1.0 + nrm((N_EVEN, RW_W), 0.05),
        'rw_rk': nrm((N_EVEN, H_RWKV, HEAD_DIM), 0.1),
        'rw_ln_g': 1.0 + nrm((N_EVEN, RW_W), 0.02),
        'rw_ln_b': nrm((N_EVEN, RW_W), 0.02),
        'sb_qn': 1.0 + nrm((N_EVEN, HEAD_DIM), 0.02),
        'sb_kn': 1.0 + nrm((N_EVEN, HEAD_DIM), 0.02),
        'w_in_odd': nrm((N_ODD, D_MODEL, IN_ODD), D_MODEL ** -0.5),
        'm_conv_w': nrm((N_ODD, CONV_W, CONV_DIM), CONV_W ** -0.5),
        'm_conv_b': nrm((N_ODD, CONV_DIM), 0.02),
        'm_dt_bias': dt0 + jnp.log(-jnp.expm1(-dt0)),
        'm_a_log': jnp.log(unif((N_ODD, H_SSM), 1.0, 16.0)),
        'm_d': 1.0 + nrm((N_ODD, H_SSM), 0.1),
        'm_norm_g': 1.0 + nrm((N_ODD, SSM_W), 0.02),
        'fox_bf': 2.0 + nrm((N_ODD, H_FOX), 0.5),
        'fox_qn': 1.0 + nrm((N_ODD, HEAD_DIM), 0.02),
        'fox_kn': 1.0 + nrm((N_ODD, HEAD_DIM), 0.02),
    }


def reference(x_prompt, x_sample, cache_sb_k, cache_sb_v, state_rwkv, state_rwkv_shift,
              cache_fox_k, cache_fox_v, cache_fox_logf, state_ssm, state_conv, page_table,
              c_prompt, c_sample, w_ada, b_ada, g_mix, g_ffn, w_out, w_ffn_gate, w_ffn_up, w_ffn_down,
              w_in_even, rw_mu, rw_w0, rw_w2, rw_a0, rw_a2, rw_g2, rw_kk, rw_ka, rw_rk, rw_ln_g, rw_ln_b,
              sb_qn, sb_kn, w_in_odd, m_conv_w, m_conv_b, m_dt_bias, m_a_log, m_d, m_norm_g,
              fox_bf, fox_qn, fox_kn):
    layer_w = dict(w_ada=w_ada, b_ada=b_ada, g_mix=g_mix, g_ffn=g_ffn, w_out=w_out,
                   w_ffn_gate=w_ffn_gate, w_ffn_up=w_ffn_up, w_ffn_down=w_ffn_down)
    even_w = dict(w_in_even=w_in_even, rw_mu=rw_mu, rw_w0=rw_w0, rw_w2=rw_w2, rw_a0=rw_a0, rw_a2=rw_a2,
                  rw_g2=rw_g2, rw_kk=rw_kk, rw_ka=rw_ka, rw_rk=rw_rk, rw_ln_g=rw_ln_g, rw_ln_b=rw_ln_b,
                  sb_qn=sb_qn, sb_kn=sb_kn)
    odd_w = dict(w_in_odd=w_in_odd, m_conv_w=m_conv_w, m_conv_b=m_conv_b, m_dt_bias=m_dt_bias,
                 m_a_log=m_a_log, m_d=m_d, m_norm_g=m_norm_g, fox_bf=fox_bf, fox_qn=fox_qn, fox_kn=fox_kn)
    caches = dict(cache_sb_k=cache_sb_k, cache_sb_v=cache_sb_v, state_rwkv=state_rwkv,
                  state_rwkv_shift=state_rwkv_shift, cache_fox_k=cache_fox_k, cache_fox_v=cache_fox_v,
                  cache_fox_logf=cache_fox_logf, state_ssm=state_ssm, state_conv=state_conv)
    y_prompt, ep, op = trunk(x_prompt, c_prompt, layer_w, even_w, odd_w, None, None)
    y_sample, es, os_ = trunk(x_sample, c_sample, layer_w, even_w, odd_w, caches, page_table)
    return (y_prompt, y_sample,
            ep['sb_k'], es['sb_k'], ep['sb_v'], es['sb_v'],
            ep['rwkv'], es['rwkv'], ep['rwkv_shift'], es['rwkv_shift'],
            op['fox_k'], os_['fox_k'], op['fox_v'], os_['fox_v'],
            op['fox_logf'], os_['fox_logf'],
            op['ssm'], os_['ssm'], op['conv'], os_['conv'])
```

```python
import functools

import jax
import jax.numpy as jnp
from jax import lax
from jax.experimental import pallas as pl
from jax.experimental.pallas import tpu as pltpu

F32 = jnp.float32
BF16 = jnp.bfloat16
HIGHEST = lax.Precision.HIGHEST

HEAD_DIM = 64
N_HEADS = 16
HW = N_HEADS * HEAD_DIM
LANE = 128
NORM_EPS = 1e-6
GN_EPS = 64e-5
LORA_PAD = 128
LORA_GATE = 256
SSM_GROUPS = 4
SSM_STATE = 128
CONV_W = 4
PAGE = 128
SUB = 16
TILE = 128
V7X_VMEM_LIMIT = 48 * 1024 * 1024

EV_R, EV_K, EV_V = 0, HW, 2 * HW
EV_SBQ, EV_SBK, EV_SBV = 3 * HW, 4 * HW, 5 * HW
EV_LORA = 6 * HW
LORA_W = 2 * LORA_PAD + LORA_GATE
RW_MAIN = 3 * HW
OD_XBC, OD_Z = 0, 2 * HW
OD_Q, OD_K, OD_V = 3 * HW, 4 * HW, 5 * HW
OD_DT, OD_F = 6 * HW, 6 * HW + LANE
PROJ_PAD = 6 * HW + LORA_W


def _cp(sems, vmem=V7X_VMEM_LIMIT):
    return pltpu.CompilerParams(dimension_semantics=sems, vmem_limit_bytes=vmem)


def _dot(a, b):
    return jnp.dot(a, b, preferred_element_type=F32)


def _dot_hi(a, b):
    return jnp.dot(a, b, preferred_element_type=F32, precision=HIGHEST)


def _dot_nt(a, b, precision=None):
    return lax.dot_general(a, b, (((1,), (1,)), ((), ())), preferred_element_type=F32,
                           precision=precision)


def _split3(x):
    hi = x.astype(BF16)
    r1 = x - hi.astype(F32)
    mid = r1.astype(BF16)
    lo = (r1 - mid.astype(F32)).astype(BF16)
    return hi, mid, lo


def _dot_x01(x, m):
    hi, mid, lo = _split3(x)
    return _dot(hi, m) + _dot(mid, m) + _dot(lo, m)


def _dot_01x(m, x):
    hi, mid, lo = _split3(x)
    return _dot(m, hi) + _dot(m, mid) + _dot(m, lo)


def _sigmoid(x):
    return jax.nn.sigmoid(x)


def _softplus(x):
    return jnp.maximum(x, 0.0) + jnp.log(1.0 + jnp.exp(-jnp.abs(x)))


def _iota2(shape, dim):
    return lax.broadcasted_iota(jnp.int32, shape, dim)


def _block_ones(n, seg):
    i, j = _iota2((n, n), 0), _iota2((n, n), 1)
    return jnp.where((i // seg) == (j // seg), 1.0, 0.0).astype(BF16)


def _ada_kernel(c_ref, w_ref, b_ref, o_ref):
    c = c_ref[...]
    a = (c * _sigmoid(c)).astype(BF16)
    o_ref[...] = _dot(a, w_ref[...].astype(BF16)) + b_ref[...]


def ada_mod(c_all, w_ada, b_ada, tn=1024):
    depth, d, n = w_ada.shape
    mp = c_all.shape[0]
    return pl.pallas_call(
        _ada_kernel,
        out_shape=jax.ShapeDtypeStruct((depth, mp, n), F32),
        grid=(depth, n // tn),
        in_specs=[pl.BlockSpec((mp, d), lambda l, j: (0, 0)),
                  pl.BlockSpec((None, d, tn), lambda l, j: (l, 0, j)),
                  pl.BlockSpec((None, 1, tn), lambda l, j: (l, 0, j))],
        out_specs=pl.BlockSpec((None, mp, tn), lambda l, j: (l, 0, j)),
        compiler_params=_cp(("parallel", "parallel")),
        name="ada_mod",
    )(c_all, w_ada, b_ada.reshape(depth, 1, n))


def _modulate(x, g, sh, sc):
    y = x * lax.rsqrt(jnp.mean(x * x, axis=-1, keepdims=True) + NORM_EPS)
    return y * g * (1.0 + sc) + sh


def _inproj_kernel(x_ref, g_ref, sh_ref, sc_ref, w_ref, o_ref, h_scr):
    @pl.when(pl.program_id(1) == 0)
    def _():
        h_scr[...] = _modulate(x_ref[...], g_ref[...], sh_ref[...], sc_ref[...]).astype(BF16)

    o_ref[...] = _dot(h_scr[...], w_ref[...])


def _row_spec(arr, tm, width):
    if arr.shape[0] == 1:
        return pl.BlockSpec((1, width), lambda i, j: (0, 0))
    return pl.BlockSpec((tm, width), lambda i, j: (i, 0))


def in_proj(x, g, sh, sc, w, tn=512):
    m, d = x.shape
    n = w.shape[1]
    tm = min(m, 512)
    return pl.pallas_call(
        _inproj_kernel,
        out_shape=jax.ShapeDtypeStruct((m, n), F32),
        grid=(m // tm, n // tn),
        in_specs=[pl.BlockSpec((tm, d), lambda i, j: (i, 0)),
                  pl.BlockSpec((1, d), lambda i, j: (0, 0)),
                  _row_spec(sh, tm, d), _row_spec(sc, tm, d),
                  pl.BlockSpec((d, tn), lambda i, j: (0, j))],
        out_specs=pl.BlockSpec((tm, tn), lambda i, j: (i, j)),
        scratch_shapes=[pltpu.VMEM((tm, d), BF16)],
        compiler_params=_cp(("parallel", "arbitrary")),
        name="in_proj",
    )(x, g, sh, sc, w)


def _outproj_kernel(a_ref, b_ref, wa_ref, wb_ref, x_ref, gt_ref, o_ref):
    acc = _dot(a_ref[...], wa_ref[...]) + _dot(b_ref[...], wb_ref[...])
    o_ref[...] = x_ref[...] + gt_ref[...] * acc


def out_proj(a, b, wa, wb, x, gt, tn=1024):
    m, ka = a.shape
    n = wa.shape[1]
    tm = min(m, 512)
    gt_spec = (pl.BlockSpec((1, tn), lambda i, j: (0, j)) if gt.shape[0] == 1
               else pl.BlockSpec((tm, tn), lambda i, j: (i, j)))
    return pl.pallas_call(
        _outproj_kernel,
        out_shape=jax.ShapeDtypeStruct((m, n), F32),
        grid=(m // tm, n // tn),
        in_specs=[pl.BlockSpec((tm, ka), lambda i, j: (i, 0)),
                  pl.BlockSpec((tm, ka), lambda i, j: (i, 0)),
                  pl.BlockSpec((ka, tn), lambda i, j: (0, j)),
                  pl.BlockSpec((ka, tn), lambda i, j: (0, j)),
                  pl.BlockSpec((tm, tn), lambda i, j: (i, j)),
                  gt_spec],
        out_specs=pl.BlockSpec((tm, tn), lambda i, j: (i, j)),
        compiler_params=_cp(("parallel", "parallel")),
        name="out_proj",
    )(a, b, wa, wb, x, gt)


def _ffn_kernel(x_ref, g_ref, sh_ref, sc_ref, gt_ref, wg_ref, wu_ref, wd_ref, o_ref, h_scr, acc_scr):
    f = pl.program_id(1)

    @pl.when(f == 0)
    def _():
        h_scr[...] = _modulate(x_ref[...], g_ref[...], sh_ref[...], sc_ref[...]).astype(BF16)
        acc_scr[...] = jnp.zeros_like(acc_scr)

    h = h_scr[...]
    gate = _dot(h, wg_ref[...])
    up = _dot(h, wu_ref[...])
    act = (gate * _sigmoid(gate) * up).astype(BF16)
    acc_scr[...] += _dot(act, wd_ref[...])

    @pl.when(f == pl.num_programs(1) - 1)
    def _():
        o_ref[...] = x_ref[...] + gt_ref[...] * acc_scr[...]


def ffn(x, g, sh, sc, gt, wg, wu, wd, tf=512):
    m, d = x.shape
    dff = wg.shape[1]
    tm = min(m, 512)
    return pl.pallas_call(
        _ffn_kernel,
        out_shape=jax.ShapeDtypeStruct((m, d), F32),
        grid=(m // tm, dff // tf),
        in_specs=[pl.BlockSpec((tm, d), lambda i, f: (i, 0)),
                  pl.BlockSpec((1, d), lambda i, f: (0, 0)),
                  _row_spec(sh, tm, d), _row_spec(sc, tm, d), _row_spec(gt, tm, d),
                  pl.BlockSpec((d, tf), lambda i, f: (0, f)),
                  pl.BlockSpec((d, tf), lambda i, f: (0, f)),
                  pl.BlockSpec((tf, d), lambda i, f: (f, 0))],
        out_specs=pl.BlockSpec((tm, d), lambda i, f: (i, 0)),
        scratch_shapes=[pltpu.VMEM((tm, d), BF16), pltpu.VMEM((tm, d), F32)],
        compiler_params=_cp(("parallel", "arbitrary")),
        name="ffn",
    )(x, g, sh, sc, gt, wg, wu, wd)


def _rwkv_prep_body(pm, pl_, ppm, ppl, prow, mum_ref, mul_ref, w0_ref, w2_ref, a0_ref, a2_ref, g2_ref, kk_ref,
                    ka_ref, rk_ref, outs):
    r_o, k_o, v_o, kap_o, a_o, lw_o, g_o, bon_o = outs
    tt = pm.shape[0]

    def lerp(cur, prev, mu, off, width):
        sl = slice(off, off + width)
        return cur[:, sl] + (prev[prow:prow + tt, sl] - cur[:, sl]) * mu[:, sl]

    xs = lambda off, width: lerp(pm, ppm, mum_ref, off, width)
    twd = jnp.tanh(lerp(pl_, ppl, mul_ref, 0, LORA_PAD)).astype(BF16)
    xad = lerp(pl_, ppl, mul_ref, LORA_PAD, LORA_PAD).astype(BF16)
    sgd = _sigmoid(lerp(pl_, ppl, mul_ref, 2 * LORA_PAD, LORA_GATE)).astype(BF16)
    bo = _block_ones(LANE, HEAD_DIM)
    for c in range(HW // LANE):
        cs = slice(c * LANE, (c + 1) * LANE)
        r = xs(EV_R + c * LANE, LANE)
        k = xs(EV_K + c * LANE, LANE)
        v = xs(EV_V + c * LANE, LANE)
        w_log = -_softplus(-(w0_ref[:, cs] + _dot(twd, w2_ref[:, cs]))) - 0.5
        a = _sigmoid(a0_ref[:, cs] + _dot(xad, a2_ref[:, cs]))
        kkv = k * kk_ref[:, cs]
        kap = kkv * lax.rsqrt(jnp.maximum(_dot_x01(kkv * kkv, bo), 1e-24))
        kmod = k * (1.0 + (a - 1.0) * ka_ref[:, cs])
        r_o[:, cs] = r
        k_o[:, cs] = kmod
        v_o[:, cs] = v
        kap_o[:, cs] = kap
        a_o[:, cs] = a
        lw_o[:, cs] = -jnp.exp(w_log)
        g_o[:, cs] = _dot(sgd, g2_ref[:, cs])
        bon_o[:, cs] = _dot_x01(r * kmod * rk_ref[:, cs], bo) * v


N_PREP_PARAMS = 10
N_PREP_OUTS = 8


def _rwkv_prep_seq_kernel(pm_ref, pl_ref, s0m_ref, s0l_ref, *rest):
    params = rest[:N_PREP_PARAMS]
    outs = rest[N_PREP_PARAMS:N_PREP_PARAMS + N_PREP_OUTS]
    extm, extl = rest[N_PREP_PARAMS + N_PREP_OUTS:]
    tt = pm_ref.shape[0]

    @pl.when(pl.program_id(0) == 0)
    def _():
        extm[0:8, :] = jnp.broadcast_to(s0m_ref[...], (8, extm.shape[1]))
        extl[0:8, :] = jnp.broadcast_to(s0l_ref[...], (8, extl.shape[1]))

    extm[8:8 + tt, :] = pm_ref[...]
    extl[8:8 + tt, :] = pl_ref[...]
    _rwkv_prep_body(pm_ref, pl_ref, extm, extl, 7, *params, outs)
    extm[0:8, :] = extm[tt:tt + 8, :]
    extl[0:8, :] = extl[tt:tt + 8, :]


def _rwkv_prep_step_kernel(pm_ref, pl_ref, prevm_ref, prevl_ref, *rest):
    params = rest[:N_PREP_PARAMS]
    outs = rest[N_PREP_PARAMS:N_PREP_PARAMS + N_PREP_OUTS]
    _rwkv_prep_body(pm_ref, pl_ref, prevm_ref, prevl_ref, 0, *params, outs)


def rwkv_prep(p, prev_main, prev_lora, prm, sequential, tt=256):
    m = p.shape[0]
    tt = min(tt, m)
    full = lambda a: pl.BlockSpec(a.shape, lambda i: (0, 0))
    params = [prm[n] for n in ("mu_main", "mu_lora", "w0", "w2", "a0", "a2", "g2", "kk", "ka", "rk")]
    assert len(params) == N_PREP_PARAMS
    pm_spec = pl.BlockSpec((tt, RW_MAIN), lambda i: (i, 0))
    pl_spec = pl.BlockSpec((tt, LORA_W), lambda i: (i, EV_LORA // LORA_W))
    if sequential:
        kern = _rwkv_prep_seq_kernel
        prev_specs = [pl.BlockSpec((1, RW_MAIN), lambda i: (0, 0)), pl.BlockSpec((1, LORA_W), lambda i: (0, 0))]
        scratch = [pltpu.VMEM((tt + 8, RW_MAIN), F32), pltpu.VMEM((tt + 8, LORA_W), F32)]
    else:
        kern = _rwkv_prep_step_kernel
        prev_specs = [pl.BlockSpec((tt, RW_MAIN), lambda i: (i, 0)), pl.BlockSpec((tt, LORA_W), lambda i: (i, 0))]
        scratch = []
    out_spec = pl.BlockSpec((tt, HW), lambda i: (i, 0))
    return pl.pallas_call(
        kern,
        out_shape=[jax.ShapeDtypeStruct((m, HW), F32)] * N_PREP_OUTS,
        grid=(m // tt,),
        in_specs=[pm_spec, pl_spec] + prev_specs + [full(a) for a in params],
        out_specs=[out_spec] * N_PREP_OUTS,
        scratch_shapes=scratch,
        compiler_params=_cp(("arbitrary",)),
        name="rwkv_prep",
    )(p, p, prev_main, prev_lora, *params)


def _rwkv_head_tile(r, k, v, kap, a, lw, h0):
    n = TILE
    row, col = _iota2((n, n), 0), _iota2((n, n), 1)
    same = (row // SUB) == (col // SUB)
    m_incl = jnp.where(same & (row >= col), 1.0, 0.0).astype(BF16)
    m_all = jnp.where(same, 1.0, 0.0).astype(BF16)
    strict = same & (row > col)
    incl = same & (row >= col)
    cum = _dot_01x(m_incl, lw)
    cend = _dot_01x(m_all, lw)
    e_pos, e_neg, e_end = jnp.exp(cum), jnp.exp(-cum), jnp.exp(cend - cum)
    b = kap * a
    kt, rt = kap * jnp.exp(cum - lw), r * e_pos
    bt, ktil = b * e_neg, k * e_neg
    bh, kh = b * e_end, k * e_end
    pend = jnp.exp(cend)
    qk = _dot_nt(jnp.concatenate([kt, rt], axis=0), jnp.concatenate([bt, ktil], axis=0), HIGHEST)
    zero = jnp.zeros((n, n), F32)
    neg_a = jnp.where(strict, -qk[:n, :n], zero)
    bm = jnp.where(strict, qk[:n, n:], zero)
    mb = jnp.where(incl, qk[n:, :n], zero)
    mk = jnp.where(incl, qk[n:, n:], zero)
    n2 = _dot_hi(neg_a, neg_a)
    n4 = _dot_hi(n2, n2)
    n8 = _dot_hi(n4, n4)
    tm = jnp.where(row == col, 1.0, 0.0) + neg_a
    tm = tm + _dot_hi(tm, n2)
    tm = tm + _dot_hi(tm, n4)
    tm = tm + _dot_hi(tm, n8)
    w = _dot_hi(tm, kt)
    ub = _dot_hi(tm, _dot_hi(bm, v))
    rp = rt - _dot_hi(mb, w)
    op = _dot_hi(mk, v) - _dot_hi(mb, ub)
    bkt = jnp.concatenate([bh, kh], axis=1).T
    bht, kht = bkt[:HEAD_DIM, :], bkt[HEAD_DIM:, :]
    lane_t = _iota2((HEAD_DIM, n), 1)
    eye = _iota2((HEAD_DIM, HEAD_DIM), 0) == _iota2((HEAD_DIM, HEAD_DIM), 1)
    h = h0
    outs = []
    for c in range(n // SUB):
        msk = (lane_t // SUB) == c
        bc = jnp.where(msk, bht, 0.0)
        kc = jnp.where(msk, kht, 0.0)
        g_c = jnp.where(eye, pend[c * SUB:c * SUB + 1, :], 0.0) - _dot_hi(bc, w)
        n_c = _dot_hi(kc, v) - _dot_hi(bc, ub)
        rows = slice(c * SUB, (c + 1) * SUB)
        res = _dot_hi(jnp.concatenate([g_c, rp[rows]], axis=0), h)
        outs.append(res[HEAD_DIM:] + op[rows])
        h = res[:HEAD_DIM] + n_c
    return jnp.concatenate(outs, axis=0), h


def _rwkv_scan_kernel(r_ref, k_ref, v_ref, kap_ref, a_ref, lw_ref, g_ref, bon_ref, lng_ref, lnb_ref,
                      o_ref, hfin_ref, h_scr):
    @pl.when(pl.program_id(1) == 0)
    def _():
        h_scr[...] = jnp.zeros_like(h_scr)

    res = []
    for hh in range(2):
        sl = slice(hh * HEAD_DIM, (hh + 1) * HEAD_DIM)
        o, h = _rwkv_head_tile(r_ref[:, sl], k_ref[:, sl], v_ref[:, sl], kap_ref[:, sl], a_ref[:, sl],
                               lw_ref[:, sl], h_scr[hh])
        h_scr[hh] = h
        hfin_ref[hh] = h
        mean = jnp.mean(o, axis=-1, keepdims=True)
        var = jnp.mean(jnp.square(o - mean), axis=-1, keepdims=True)
        res.append((o - mean) * lax.rsqrt(var + GN_EPS))
    on = jnp.concatenate(res, axis=1) * lng_ref[...] + lnb_ref[...]
    o_ref[...] = ((on + bon_ref[...]) * g_ref[...]).astype(o_ref.dtype)


def rwkv_scan(r, k, v, kap, a, lw, g, bon, lng, lnb):
    t = r.shape[0]
    tile = pl.BlockSpec((TILE, LANE), lambda p, i: (i, p))
    vec = pl.BlockSpec((1, LANE), lambda p, i: (0, p))
    return pl.pallas_call(
        _rwkv_scan_kernel,
        out_shape=[jax.ShapeDtypeStruct((t, HW), BF16),
                   jax.ShapeDtypeStruct((N_HEADS, HEAD_DIM, HEAD_DIM), F32)],
        grid=(HW // LANE, t // TILE),
        in_specs=[tile] * 8 + [vec, vec],
        out_specs=[tile, pl.BlockSpec((2, HEAD_DIM, HEAD_DIM), lambda p, i: (p, 0, 0))],
        scratch_shapes=[pltpu.VMEM((2, HEAD_DIM, HEAD_DIM), F32)],
        compiler_params=_cp(("parallel", "arbitrary")),
        name="rwkv_scan",
    )(r, k, v, kap, a, lw, g, bon, lng, lnb)


def _col_bcast(rows, width):
    nr = rows.shape[0]
    eye = _iota2((HEAD_DIM, HEAD_DIM), 0) == _iota2((HEAD_DIM, HEAD_DIM), 1)
    diag = jnp.concatenate([jnp.where(eye, rows[j:j + 1, :], 0.0) for j in range(nr)], axis=0)
    return _dot_x01(diag, jnp.ones((HEAD_DIM, width), BF16))


def _rwkv_step_kernel(s_ref, r_ref, k_ref, v_ref, kap_ref, a_ref, lw_ref, g_ref, bon_ref, lng_ref, lnb_ref,
                      o_ref, so_ref):
    nb = s_ref.shape[0]
    bo = _block_ones(LANE, HEAD_DIM)
    for bi in range(nb):
        pieces = []
        for h in range(N_HEADS):
            sl = slice(h * HEAD_DIM, (h + 1) * HEAD_DIM)
            row = lambda ref: ref[bi:bi + 1, sl]
            s = s_ref[bi, h]
            kap, w = row(kap_ref), jnp.exp(row(lw_ref))
            sa = jnp.sum(s * kap, axis=-1, keepdims=True)
            vcol = _col_bcast(row(v_ref), HEAD_DIM)
            s_new = s * w - sa * (kap * row(a_ref)) + vcol * row(k_ref)
            so_ref[bi, h] = s_new
            r8 = jnp.broadcast_to(row(r_ref), (8, HEAD_DIM))
            pieces.append(_dot_nt(r8, s_new, HIGHEST)[0:1, :])
        o = jnp.concatenate(pieces, axis=1)
        res = []
        for c in range(HW // LANE):
            oc = o[:, c * LANE:(c + 1) * LANE]
            o8 = jnp.broadcast_to(oc, (8, LANE))
            mean = _dot_x01(o8, bo) * (1.0 / HEAD_DIM)
            cen = o8 - mean
            var = _dot_x01(cen * cen, bo) * (1.0 / HEAD_DIM)
            res.append((cen * lax.rsqrt(var + GN_EPS))[0:1, :])
        on = jnp.concatenate(res, axis=1) * lng_ref[...] + lnb_ref[...]
        o_ref[bi:bi + 1, :] = ((on + bon_ref[bi:bi + 1, :]) * g_ref[bi:bi + 1, :]).astype(o_ref.dtype)


def rwkv_step(s0, r, k, v, kap, a, lw, g, bon, lng, lnb, nb=8):
    b = s0.shape[0]
    st = pl.BlockSpec((nb, N_HEADS, HEAD_DIM, HEAD_DIM), lambda i: (i, 0, 0, 0))
    rows = pl.BlockSpec((nb, HW), lambda i: (i, 0))
    vec = pl.BlockSpec((1, HW), lambda i: (0, 0))
    return pl.pallas_call(
        _rwkv_step_kernel,
        out_shape=[jax.ShapeDtypeStruct((b, HW), F32), jax.ShapeDtypeStruct(s0.shape, F32)],
        grid=(b // nb,),
        in_specs=[st] + [rows] * 8 + [vec, vec],
        out_specs=[rows, st],
        compiler_params=_cp(("parallel",)),
        name="rwkv_step",
    )(s0, r, k, v, kap, a, lw, g, bon, lng, lnb)


def _head_rmsnorm(x, gain, bo):
    ms = _dot_x01(x * x, bo) * (1.0 / HEAD_DIM)
    return x * lax.rsqrt(ms + NORM_EPS) * gain


def _qknorm_kernel(q_ref, k_ref, qg_ref, kg_ref, qo_ref, ko_ref):
    bo = _block_ones(LANE, HEAD_DIM)
    for c in range(HW // LANE):
        cs = slice(c * LANE, (c + 1) * LANE)
        qo_ref[:, cs] = (_head_rmsnorm(q_ref[:, cs], qg_ref[...], bo) * (HEAD_DIM ** -0.5)).astype(qo_ref.dtype)
        ko_ref[:, cs] = _head_rmsnorm(k_ref[:, cs], kg_ref[...], bo)


def qk_norm(p, q_off, qg, kg, tt=256):
    m = p.shape[0]
    tt = min(tt, m)
    qb, kb = q_off // HW, q_off // HW + 1
    assert q_off % HW == 0
    gain = pl.BlockSpec((1, LANE), lambda i: (0, 0))
    out = pl.BlockSpec((tt, HW), lambda i: (i, 0))
    return pl.pallas_call(
        _qknorm_kernel,
        out_shape=[jax.ShapeDtypeStruct((m, HW), BF16), jax.ShapeDtypeStruct((m, HW), F32)],
        grid=(m // tt,),
        in_specs=[pl.BlockSpec((tt, HW), lambda i: (i, qb)), pl.BlockSpec((tt, HW), lambda i: (i, kb)),
                  gain, gain],
        out_specs=[out, out],
        compiler_params=_cp(("parallel",)),
        name="qk_norm",
    )(p, p, qg, kg)


def _logf_kernel(f_ref, bf_ref, lf_ref, fc_ref, carry):
    tt = f_ref.shape[0]

    @pl.when(pl.program_id(0) == 0)
    def _():
        carry[...] = jnp.zeros_like(carry)

    lf = -_softplus(-(f_ref[...] + bf_ref[...]))
    lf_ref[...] = lf
    lower = jnp.where(_iota2((tt, tt), 0) >= _iota2((tt, tt), 1), 1.0, 0.0).astype(BF16)
    fc = _dot_01x(lower, lf) + carry[0:1, :]
    fc_ref[...] = fc
    carry[...] = jnp.broadcast_to(fc[tt - 1:tt, :], carry.shape)


def logf_cumsum(p, f_off, bf_pad, tt=128):
    m = p.shape[0]
    blk = pl.BlockSpec((tt, LANE), lambda i: (i, 0))
    return pl.pallas_call(
        _logf_kernel,
        out_shape=[jax.ShapeDtypeStruct((m, LANE), F32)] * 2,
        grid=(m // tt,),
        in_specs=[pl.BlockSpec((tt, LANE), lambda i: (i, f_off // LANE)),
                  pl.BlockSpec((1, LANE), lambda i: (0, 0))],
        out_specs=[blk, blk],
        scratch_shapes=[pltpu.VMEM((8, LANE), F32)],
        compiler_params=_cp(("arbitrary",)),
        name="logf_cumsum",
    )(p, bf_pad)


SB_TQ = 256
SB_TK = 128


def _sb_block(q, kb, vb, carry, u2, mask):
    z = _dot_nt(q, kb)
    lk = -_softplus(z)
    if mask is not None:
        lk = jnp.where(mask, lk, 0.0)
    cs = _dot_x01(lk, u2)
    tk = kb.shape[0]
    wgt = jnp.exp(lk + z + cs[:, :tk] + carry)
    if mask is not None:
        wgt = jnp.where(mask, wgt, 0.0)
    return _dot(wgt.astype(BF16), vb), carry + cs[:, tk:]


def _sb_attn_kernel(q_ref, k_ref, v_ref, o_ref):
    i = pl.program_id(1)
    tq, tk = SB_TQ, SB_TK
    r, c = _iota2((tk, 2 * tk), 0), _iota2((tk, 2 * tk), 1)
    u2 = jnp.where((c >= tk) | (r > c), 1.0, 0.0).astype(BF16)
    ratio = tq // tk
    outs = []
    for hh in range(2):
        sl = slice(hh * HEAD_DIM, (hh + 1) * HEAD_DIM)
        q = q_ref[:, sl]
        acc = jnp.zeros((tq, HEAD_DIM), F32)
        carry = jnp.zeros((tq, tk), F32)
        for d in range(ratio):
            start = pl.multiple_of((i * ratio + (ratio - 1 - d)) * tk, tk)
            kb = k_ref[pl.ds(start, tk), sl].astype(BF16)
            vb = v_ref[pl.ds(start, tk), sl].astype(BF16)
            qpos = _iota2((tq, tk), 0)
            kpos = _iota2((tq, tk), 1) + (ratio - 1 - d) * tk
            part, carry = _sb_block(q, kb, vb, carry, u2, kpos < qpos)
            acc = acc + part

        def body(s, st):
            acc, carry = st
            start = pl.multiple_of((i * ratio - 1 - s) * tk, tk)
            kb = k_ref[pl.ds(start, tk), sl].astype(BF16)
            vb = v_ref[pl.ds(start, tk), sl].astype(BF16)
            part, carry = _sb_block(q, kb, vb, carry, u2, None)
            return acc + part, carry

        acc, carry = lax.fori_loop(0, i * ratio, body, (acc, carry))
        outs.append(acc)
    o_ref[...] = jnp.concatenate(outs, axis=1).astype(o_ref.dtype)


def sb_attention(q, k, v):
    t = q.shape[0]
    seq = pl.BlockSpec((t, LANE), lambda p, i: (0, p))
    blk = pl.BlockSpec((SB_TQ, LANE), lambda p, i: (i, p))
    return pl.pallas_call(
        _sb_attn_kernel,
        out_shape=jax.ShapeDtypeStruct((t, HW), BF16),
        grid=(HW // LANE, t // SB_TQ),
        in_specs=[blk, seq, seq],
        out_specs=blk,
        compiler_params=_cp(("parallel", "arbitrary")),
        name="sb_attention",
    )(q, k, v)


FOX_TQ = 256
FOX_TK = 128


def _fox_block(q, kb, vb, fq, fk, m, l, acc, mask):
    s = _dot_nt(q, kb) + (fq - fk)
    if mask is not None:
        s = jnp.where(mask, s, -jnp.inf)
    m_new = jnp.maximum(m, jnp.max(s, axis=-1, keepdims=True))
    alpha = jnp.exp(m - m_new)
    pr = jnp.exp(s - m_new)
    l = alpha * l + jnp.sum(pr, axis=-1, keepdims=True)
    acc = alpha * acc + _dot(pr.astype(BF16), vb)
    return m_new, l, acc


def _fox_attn_kernel(q_ref, k_ref, v_ref, fq_ref, fk_ref, o_ref):
    i = pl.program_id(1)
    tq, tk = FOX_TQ, FOX_TK
    ratio = tq // tk
    outs = []
    for hh in range(2):
        sl = slice(hh * HEAD_DIM, (hh + 1) * HEAD_DIM)
        q = q_ref[:, sl]
        fq = fq_ref[hh]
        m = jnp.full((tq, 1), -jnp.inf, F32)
        l = jnp.zeros((tq, 1), F32)
        acc = jnp.zeros((tq, HEAD_DIM), F32)
        for d in range(ratio):
            j = i * ratio + d
            start = pl.multiple_of(j * tk, tk)
            kb = k_ref[pl.ds(start, tk), sl].astype(BF16)
            vb = v_ref[pl.ds(start, tk), sl].astype(BF16)
            fk = fk_ref[hh, pl.ds(j, 1), :]
            qpos = _iota2((tq, tk), 0)
            kpos = _iota2((tq, tk), 1) + d * tk
            m, l, acc = _fox_block(q, kb, vb, fq, fk, m, l, acc, kpos <= qpos)

        def body(j, st):
            m, l, acc = st
            start = pl.multiple_of(j * tk, tk)
            kb = k_ref[pl.ds(start, tk), sl].astype(BF16)
            vb = v_ref[pl.ds(start, tk), sl].astype(BF16)
            fk = fk_ref[hh, pl.ds(j, 1), :]
            return _fox_block(q, kb, vb, fq, fk, m, l, acc, None)

        m, l, acc = lax.fori_loop(0, i * ratio, body, (m, l, acc))
        outs.append(acc / l)
    o_ref[...] = jnp.concatenate(outs, axis=1).astype(o_ref.dtype)


def fox_attention(q, k, v, f_col, f_row):
    t = q.shape[0]
    seq = pl.BlockSpec((t, LANE), lambda p, i: (0, p))
    blk = pl.BlockSpec((FOX_TQ, LANE), lambda p, i: (i, p))
    return pl.pallas_call(
        _fox_attn_kernel,
        out_shape=jax.ShapeDtypeStruct((t, HW), BF16),
        grid=(HW // LANE, t // FOX_TQ),
        in_specs=[blk, seq, seq,
                  pl.BlockSpec((2, FOX_TQ, 1), lambda p, i: (p, i, 0)),
                  pl.BlockSpec((2, t // FOX_TK, FOX_TK), lambda p, i: (p, 0, 0))],
        out_specs=blk,
        compiler_params=_cp(("parallel", "arbitrary")),
        name="fox_attention",
    )(q, k, v, f_col, f_row)


PAGES_PER_STEP = 4


def _suffix_sum(x):
    n = x.shape[-1]
    lane = _iota2(x.shape, x.ndim - 1)
    y, d = x, 1
    while d < n:
        y = y + jnp.where(lane + d < n, pltpu.roll(y, n - d, axis=x.ndim - 1), 0.0)
        d *= 2
    return y


def _head_diag(acc):
    keep = (_iota2(acc.shape, 1) // HEAD_DIM) == _iota2(acc.shape, 0)
    return jnp.sum(jnp.where(keep, acc, 0.0), axis=0, keepdims=True)


def _sb_decode_kernel(pt_ref, qm_ref, *rest):
    del pt_ref
    n = PAGES_PER_STEP
    k_refs, v_refs, o_ref, acc, carry = rest[:n], rest[n:2 * n], rest[2 * n], rest[2 * n + 1], rest[2 * n + 2]
    g = pl.program_id(1)

    @pl.when(g == 0)
    def _():
        acc[...] = jnp.zeros_like(acc)
        carry[...] = jnp.zeros_like(carry)

    qm = qm_ref[...]
    for s in range(n):
        z = _dot_nt(qm, k_refs[s][...].astype(BF16))
        lk = -_softplus(z)
        suf = _suffix_sum(lk)
        wgt = jnp.exp(z + suf + carry[...])
        acc[...] += _dot(wgt.astype(BF16), v_refs[s][...].astype(BF16))
        carry[...] += suf[:, 0:1]

    @pl.when(g == pl.num_programs(1) - 1)
    def _():
        o_ref[...] = _head_diag(acc[...])


def _page_specs(n_pages):
    n = PAGES_PER_STEP
    specs = []
    for s in range(n):
        specs.append(pl.BlockSpec(
            (None, PAGE, HW), lambda b, g, pt, s=s: (pt[b, n_pages - 1 - (g * n + s)], 0, 0)))
    return specs


def sb_decode(qm, cache_k, cache_v, page_table):
    b, n_pages = page_table.shape
    n = PAGES_PER_STEP
    grid_spec = pltpu.PrefetchScalarGridSpec(
        num_scalar_prefetch=1,
        grid=(b, n_pages // n),
        in_specs=[pl.BlockSpec((None, N_HEADS, HW), lambda bb, g, pt: (bb, 0, 0))]
                 + _page_specs(n_pages) + _page_specs(n_pages),
        out_specs=pl.BlockSpec((None, 1, HW), lambda bb, g, pt: (bb, 0, 0)),
        scratch_shapes=[pltpu.VMEM((N_HEADS, HW), F32), pltpu.VMEM((N_HEADS, 1), F32)],
    )
    return pl.pallas_call(
        _sb_decode_kernel,
        out_shape=jax.ShapeDtypeStruct((b, 1, HW), F32),
        grid_spec=grid_spec,
        compiler_params=_cp(("parallel", "arbitrary")),
        name="sb_decode",
    )(page_table, qm, *([cache_k] * n), *([cache_v] * n))


def _fox_decode_kernel(pt_ref, qm_ref, kn_ref, vn_ref, lfn_ref, *rest):
    del pt_ref
    n = PAGES_PER_STEP
    k_refs, v_refs, lf_refs = rest[:n], rest[n:2 * n], rest[2 * n:3 * n]
    o_ref, acc, m_scr, l_scr, carry = rest[3 * n:]
    g = pl.program_id(1)
    qm = qm_ref[...]

    @pl.when(g == 0)
    def _():
        m_scr[...] = jnp.sum(qm.astype(F32) * kn_ref[...], axis=-1, keepdims=True)
        l_scr[...] = jnp.ones_like(l_scr)
        acc[...] = jnp.broadcast_to(vn_ref[...], acc.shape)
        carry[...] = lfn_ref[...]

    for s in range(n):
        lf = lf_refs[s][...]
        suf = _suffix_sum(lf)
        sc = _dot_nt(qm, k_refs[s][...].astype(BF16)) + (suf - lf) + carry[...]
        m_new = jnp.maximum(m_scr[...], jnp.max(sc, axis=-1, keepdims=True))
        alpha = jnp.exp(m_scr[...] - m_new)
        pr = jnp.exp(sc - m_new)
        l_scr[...] = alpha * l_scr[...] + jnp.sum(pr, axis=-1, keepdims=True)
        acc[...] = alpha * acc[...] + _dot(pr.astype(BF16), v_refs[s][...].astype(BF16))
        m_scr[...] = m_new
        carry[...] += suf[:, 0:1]

    @pl.when(g == pl.num_programs(1) - 1)
    def _():
        o_ref[...] = _head_diag(acc[...] / l_scr[...])


def fox_decode(qm, k_new, v_new, lf_new, cache_k, cache_v, cache_lf_t, page_table):
    b, n_pages = page_table.shape
    n = PAGES_PER_STEP
    lf_specs = [pl.BlockSpec((None, N_HEADS, PAGE),
                             lambda bb, g, pt, s=s: (pt[bb, n_pages - 1 - (g * n + s)], 0, 0))
                for s in range(n)]
    row = pl.BlockSpec((None, 1, HW), lambda bb, g, pt: (bb, 0, 0))
    grid_spec = pltpu.PrefetchScalarGridSpec(
        num_scalar_prefetch=1,
        grid=(b, n_pages // n),
        in_specs=[pl.BlockSpec((None, N_HEADS, HW), lambda bb, g, pt: (bb, 0, 0)), row, row,
                  pl.BlockSpec((None, N_HEADS, 1), lambda bb, g, pt: (bb, 0, 0))]
                 + _page_specs(n_pages) + _page_specs(n_pages) + lf_specs,
        out_specs=row,
        scratch_shapes=[pltpu.VMEM((N_HEADS, HW), F32), pltpu.VMEM((N_HEADS, 1), F32),
                        pltpu.VMEM((N_HEADS, 1), F32), pltpu.VMEM((N_HEADS, 1), F32)],
    )
    return pl.pallas_call(
        _fox_decode_kernel,
        out_shape=jax.ShapeDtypeStruct((b, 1, HW), F32),
        grid_spec=grid_spec,
        compiler_params=_cp(("parallel", "arbitrary")),
        name="fox_decode",
    )(page_table, qm, k_new, v_new, lf_new, *([cache_k] * n), *([cache_v] * n), *([cache_lf_t] * n))


def _expand_heads():
    return jnp.where(_iota2((LANE, HW), 0) == _iota2((LANE, HW), 1) // HEAD_DIM, 1.0, 0.0).astype(BF16)


def _group_rmsnorm(y, gain):
    gw = HW // SSM_GROUPS
    parts = []
    for g in range(SSM_GROUPS):
        yg = y[:, g * gw:(g + 1) * gw]
        parts.append(yg * lax.rsqrt(jnp.mean(yg * yg, axis=-1, keepdims=True) + NORM_EPS))
    return jnp.concatenate(parts, axis=1) * gain


def _mamba_seq_kernel(z_ref, xbc_ref, dt_ref, dtt_ref, cw_ref, cb_ref, dtb_ref, dtbc_ref, alog_ref, alogc_ref,
                      dexp_ref, ng_ref, o_ref, hfin_ref, ext, h_scr):
    n = TILE
    ah_row, ah_col = -jnp.exp(alog_ref[...]), -jnp.exp(alogc_ref[...])

    @pl.when(pl.program_id(0) == 0)
    def _():
        ext[0:8, :] = jnp.zeros((8, ext.shape[1]), F32)
        h_scr[...] = jnp.zeros_like(h_scr)

    ext[8:8 + n, :] = xbc_ref[...]
    conv = cb_ref[...] + ext[5:5 + n, :] * cw_ref[0:1, :]
    for j in range(1, CONV_W):
        conv = conv + ext[5 + j:5 + j + n, :] * cw_ref[j:j + 1, :]
    ext[0:8, :] = ext[n:n + 8, :]
    act = conv * _sigmoid(conv)
    xs = act[:, :HW]
    gs = SSM_STATE
    dt = _softplus(dt_ref[...] + dtb_ref[...])
    a = dt * ah_row
    a_t = _softplus(dtt_ref[...] + dtbc_ref[...]) * ah_col
    row, col = _iota2((n, n), 0), _iota2((n, n), 1)
    lower = jnp.where(row >= col, 1.0, 0.0).astype(BF16)
    upper = jnp.where(row <= col, 1.0, 0.0).astype(BF16)
    a_cum = _dot_01x(lower, a)
    a_cum_t = _dot_x01(a_t, upper)
    ex = _expand_heads()
    a_cum_e = _dot_x01(a_cum, ex)
    xdt = xs * _dot_x01(dt, ex)
    a_last_e = a_cum_e[n - 1:n, :]
    x_end = xdt * jnp.exp(a_last_e - a_cum_e)
    e_cum = jnp.exp(a_cum_e)
    causal = row >= col
    ys = []
    for h in range(N_HEADS):
        g = h // (N_HEADS // SSM_GROUPS)
        sl = slice(h * HEAD_DIM, (h + 1) * HEAD_DIM)
        bm = act[:, HW + g * gs:HW + (g + 1) * gs].astype(BF16)
        cm = act[:, HW + SSM_GROUPS * gs + g * gs:HW + SSM_GROUPS * gs + (g + 1) * gs].astype(BF16)
        seg = a_cum[:, h:h + 1] - a_cum_t[h:h + 1, :]
        decay = jnp.where(causal, jnp.exp(jnp.where(causal, seg, 0.0)), 0.0)
        scores = _dot_nt(cm, bm) * decay
        hs = h_scr[h]
        y = _dot(scores.astype(BF16), xdt[:, sl].astype(BF16))
        y = y + _dot_nt(cm, hs.astype(BF16)) * e_cum[:, sl]
        ys.append(y)
        s_chunk = lax.dot_general(x_end[:, sl].astype(BF16), bm, (((0,), (0,)), ((), ())),
                                  preferred_element_type=F32)
        h_new = hs * jnp.exp(a_cum_t[h:h + 1, n - 1:n]) + s_chunk
        h_scr[h] = h_new
        hfin_ref[h] = h_new
    y = jnp.concatenate(ys, axis=1) + dexp_ref[...] * xs
    zz = z_ref[...]
    y = y * (zz * _sigmoid(zz))
    o_ref[...] = _group_rmsnorm(y, ng_ref[...]).astype(o_ref.dtype)


MAMBA_SEQ_PARAMS = ("conv_w", "conv_b", "dtb_row", "dtb_col", "alog_row", "alog_col", "d_exp", "norm_g")


def mamba_seq(p, dt_t, prm):
    t = p.shape[0]
    full = lambda a: pl.BlockSpec(a.shape, lambda i: (0,) * a.ndim)
    params = [prm[n] for n in MAMBA_SEQ_PARAMS]
    return pl.pallas_call(
        _mamba_seq_kernel,
        out_shape=[jax.ShapeDtypeStruct((t, HW), BF16),
                   jax.ShapeDtypeStruct((N_HEADS, HEAD_DIM, SSM_STATE), F32)],
        grid=(t // TILE,),
        in_specs=[pl.BlockSpec((TILE, HW), lambda i: (i, OD_Z // HW)),
                  pl.BlockSpec((TILE, 2 * HW), lambda i: (i, OD_XBC // (2 * HW))),
                  pl.BlockSpec((TILE, LANE), lambda i: (i, OD_DT // LANE)),
                  pl.BlockSpec((N_HEADS, TILE), lambda i: (0, i))] + [full(a) for a in params],
        out_specs=[pl.BlockSpec((TILE, HW), lambda i: (i, 0)),
                   pl.BlockSpec((N_HEADS, HEAD_DIM, SSM_STATE), lambda i: (0, 0, 0))],
        scratch_shapes=[pltpu.VMEM((TILE + 8, 2 * HW), F32),
                        pltpu.VMEM((N_HEADS, HEAD_DIM, SSM_STATE), F32)],
        compiler_params=_cp(("arbitrary",)),
        name="mamba_seq",
    )(p, p, p, dt_t, *params)


def _mamba_step_kernel(h_ref, cp_ref, z_ref, xbc_ref, dt_ref, cw_ref, cb_ref, dtb_ref, alog_ref, dexp_ref,
                       ng_ref, o_ref, ho_ref):
    nb = h_ref.shape[0]
    ex = _expand_heads()
    ah = -jnp.exp(alog_ref[...])
    gs = SSM_STATE
    for bi in range(nb):
        conv = cb_ref[...] + xbc_ref[bi:bi + 1, :] * cw_ref[CONV_W - 1:CONV_W, :]
        for j in range(CONV_W - 1):
            conv = conv + cp_ref[bi, j:j + 1, :] * cw_ref[j:j + 1, :]
        act = conv * _sigmoid(conv)
        xs = act[:, :HW]
        dt = _softplus(dt_ref[bi:bi + 1, :] + dtb_ref[...])
        da = jnp.exp(dt * ah)
        dd = jnp.concatenate([dt, da, jnp.zeros((6, LANE), F32)], axis=0)
        dde = _dot_x01(dd, ex)
        xdt = xs * dde[0:1, :]
        ys = []
        for h in range(N_HEADS):
            g = h // (N_HEADS // SSM_GROUPS)
            sl = slice(h * HEAD_DIM, (h + 1) * HEAD_DIM)
            bm = act[:, HW + g * gs:HW + (g + 1) * gs]
            cm = act[:, HW + (SSM_GROUPS + g) * gs:HW + (SSM_GROUPS + g + 1) * gs]
            cols = _col_bcast(jnp.concatenate([xdt[:, sl], dde[1:2, sl]], axis=0), gs)
            h_new = h_ref[bi, h] * cols[HEAD_DIM:] + cols[:HEAD_DIM] * bm
            ho_ref[bi, h] = h_new
            ys.append(_dot_nt(jnp.broadcast_to(cm, (8, gs)), h_new, HIGHEST)[0:1, :])
        y = jnp.concatenate(ys, axis=1) + dexp_ref[...] * xs
        zz = z_ref[bi:bi + 1, :]
        y = y * (zz * _sigmoid(zz))
        o_ref[bi:bi + 1, :] = _group_rmsnorm(y, ng_ref[...])


MAMBA_STEP_PARAMS = ("conv_w", "conv_b", "dtb_row", "alog_row", "d_exp", "norm_g")


def mamba_step(h0, conv_prev, p, prm, nb=8):
    b = h0.shape[0]
    full = lambda a: pl.BlockSpec(a.shape, lambda i: (0,) * a.ndim)
    params = [prm[n] for n in MAMBA_STEP_PARAMS]
    st = pl.BlockSpec((nb, N_HEADS, HEAD_DIM, SSM_STATE), lambda i: (i, 0, 0, 0))
    return pl.pallas_call(
        _mamba_step_kernel,
        out_shape=[jax.ShapeDtypeStruct((b, HW), F32), jax.ShapeDtypeStruct(h0.shape, F32)],
        grid=(b // nb,),
        in_specs=[st, pl.BlockSpec((nb, CONV_W - 1, 2 * HW), lambda i: (i, 0, 0)),
                  pl.BlockSpec((nb, HW), lambda i: (i, OD_Z // HW)),
                  pl.BlockSpec((nb, 2 * HW), lambda i: (i, OD_XBC // (2 * HW))),
                  pl.BlockSpec((nb, LANE), lambda i: (i, OD_DT // LANE))] + [full(a) for a in params],
        out_specs=[pl.BlockSpec((nb, HW), lambda i: (i, 0)), st],
        compiler_params=_cp(("parallel",)),
        name="mamba_step",
    )(h0, conv_prev, p, p, p, *params)


LORA_R = 96
RW_PROJ = 3 * HW + 2 * LORA_R + LORA_GATE
N_DT = N_HEADS


def _pad_cols(a, width):
    return jnp.pad(a, [(0, 0)] * (a.ndim - 1) + [(0, width - a.shape[-1])])


def _pad_rows(a, rows):
    return jnp.pad(a, [(0, rows - a.shape[0])] + [(0, 0)] * (a.ndim - 1))


def _rw_split(a):
    o = 3 * HW
    lora = jnp.concatenate([_pad_cols(a[..., o:o + LORA_R], LORA_PAD),
                            _pad_cols(a[..., o + LORA_R:o + 2 * LORA_R], LORA_PAD),
                            a[..., o + 2 * LORA_R:RW_PROJ]], axis=-1)
    return a[..., :o], lora


def _rw_unsplit(p):
    o = EV_LORA
    return jnp.concatenate([p[:, :3 * HW], p[:, o:o + LORA_R], p[:, o + LORA_PAD:o + LORA_PAD + LORA_R],
                            p[:, o + 2 * LORA_PAD:o + LORA_W]], axis=1)


def _even_in_weight(w):
    main, lora = _rw_split(w[:, :RW_PROJ])
    return jnp.concatenate([main, w[:, RW_PROJ:], lora], axis=1).astype(BF16)


def _odd_in_weight(w):
    ssm_proj = HW + 2 * HW + N_DT
    z, xbc, dt = w[:, :HW], w[:, HW:3 * HW], w[:, 3 * HW:ssm_proj]
    qkv, f = w[:, ssm_proj:ssm_proj + 3 * HW], w[:, ssm_proj + 3 * HW:]
    out = jnp.concatenate([xbc, z, qkv, _pad_cols(dt, LANE), _pad_cols(f, LANE)], axis=1)
    return _pad_cols(out, PROJ_PAD).astype(BF16)


def _block_diag_query(q):
    head = jnp.arange(HW, dtype=jnp.int32) // HEAD_DIM
    keep = head[None, None, :] == jnp.arange(N_HEADS, dtype=jnp.int32)[None, :, None]
    return jnp.where(keep, q[:, None, :], jnp.zeros((), q.dtype))


def _row(a):
    return a.reshape(1, -1)


def _heads(a):
    return a.reshape(a.shape[0], N_HEADS, HEAD_DIM)


def _even_layer(x, mod, w, state, page_table):
    sh1, sc1, gt1, sh2, sc2, gt2 = mod
    p = in_proj(x, w["g_mix"], sh1, sc1, w["w_in"])
    if state is None:
        prep = rwkv_prep(p, jnp.zeros((1, RW_MAIN), F32), jnp.zeros((1, LORA_W), F32), w, True)
        o_rw, hfin = rwkv_scan(*prep, w["ln_g"], w["ln_b"])
        s_new = jnp.swapaxes(hfin, 1, 2)[None]
        shift_new = _rw_unsplit(p[-1:])
    else:
        sm, sl = _rw_split(state["shift"])
        prep = rwkv_prep(p, sm, sl, w, False)
        o_rw, s_new = rwkv_step(state["rwkv"], *prep, w["ln_g"], w["ln_b"])
        o_rw = o_rw.astype(BF16)
        shift_new = _rw_unsplit(p)
    qs, kn = qk_norm(p, EV_SBQ, w["qn"], w["kn"])
    vv = p[:, EV_SBV:EV_SBV + HW]
    if state is None:
        o_sb = sb_attention(qs, kn, vv)
    else:
        o_sb = sb_decode(_block_diag_query(qs), state["sb_k"], state["sb_v"], page_table)[:, 0].astype(BF16)
    x = out_proj(o_rw, o_sb, w["wo_a"], w["wo_b"], x, gt1)
    x = ffn(x, w["g_ffn"], sh2, sc2, gt2, w["wg"], w["wu"], w["wd"])
    return x, dict(k=kn, v=vv, rwkv=s_new, shift=shift_new)


def _odd_layer(x, mod, w, state, page_table):
    sh1, sc1, gt1, sh2, sc2, gt2 = mod
    p = in_proj(x, w["g_mix"], sh1, sc1, w["w_in"])
    m = p.shape[0]
    if state is None:
        o_ssm, ssm_new = mamba_seq(p, p[:, OD_DT:OD_DT + N_DT].T, w)
        conv_new = p[m - (CONV_W - 1):, OD_XBC:OD_XBC + 2 * HW]
    else:
        o_ssm, ssm_new = mamba_step(state["ssm"], state["conv"], p, w)
        o_ssm = o_ssm.astype(BF16)
        conv_new = jnp.concatenate([state["conv"][:, 1:], p[:, None, OD_XBC:OD_XBC + 2 * HW]], axis=1)
    qs, kn = qk_norm(p, OD_Q, w["qn"], w["kn"])
    vv = p[:, OD_V:OD_V + HW]
    lf, fc = logf_cumsum(p, OD_F, w["bf"], tt=min(m, 128))
    lf = lf[:, :N_HEADS]
    if state is None:
        fct = fc[:, :N_HEADS].T
        o_fox = fox_attention(qs, kn, vv, fct[:, :, None], fct.reshape(N_HEADS, m // FOX_TK, FOX_TK))
    else:
        o_fox = fox_decode(_block_diag_query(qs), kn[:, None, :], vv[:, None, :], lf[:, :, None],
                           state["fox_k"], state["fox_v"], state["fox_lf_t"], page_table)[:, 0].astype(BF16)
    x = out_proj(o_ssm, o_fox, w["wo_a"], w["wo_b"], x, gt1)
    x = ffn(x, w["g_ffn"], sh2, sc2, gt2, w["wg"], w["wu"], w["wd"])
    return x, dict(k=kn, v=vv, logf=lf, ssm=ssm_new, conv=conv_new)


def kernel(x_prompt, x_sample, cache_sb_k, cache_sb_v, state_rwkv, state_rwkv_shift, cache_fox_k, cache_fox_v, cache_fox_logf, state_ssm, state_conv, page_table, c_prompt, c_sample, w_ada, b_ada, g_mix, g_ffn, w_out, w_ffn_gate, w_ffn_up, w_ffn_down, w_in_even, rw_mu, rw_w0, rw_w2, rw_a0, rw_a2, rw_g2, rw_kk, rw_ka, rw_rk, rw_ln_g, rw_ln_b, sb_qn, sb_kn, w_in_odd, m_conv_w, m_conv_b, m_dt_bias, m_a_log, m_d, m_norm_g, fox_bf, fox_qn, fox_kn):
    nb, t, d = x_prompt.shape
    bs = x_sample.shape[0]
    assert nb == 1 and x_sample.shape[1] == 1 and w_ada.shape[0] == 2
    n_pool = cache_sb_k.shape[1]

    rows = 8 * ((1 + bs + 7) // 8)
    c_all = jnp.concatenate([c_prompt, c_sample, jnp.zeros((rows - 1 - bs, d), F32)], axis=0)
    mod = ada_mod(c_all, w_ada, b_ada)
    mod_p = lambda l: [mod[l, 0:1, i * d:(i + 1) * d] for i in range(6)]
    mod_s = lambda l: [mod[l, 1:1 + bs, i * d:(i + 1) * d] for i in range(6)]

    def common(l):
        return dict(g_mix=_row(g_mix[l]), g_ffn=_row(g_ffn[l]),
                    wo_a=w_out[l, :HW].astype(BF16), wo_b=w_out[l, HW:].astype(BF16),
                    wg=w_ffn_gate[l].astype(BF16), wu=w_ffn_up[l].astype(BF16), wd=w_ffn_down[l].astype(BF16))

    mu_main, mu_lora = _rw_split(_row(rw_mu[0]))
    tile2 = lambda g: _row(jnp.concatenate([g, g]))
    w_even = dict(common(0), w_in=_even_in_weight(w_in_even[0]), mu_main=mu_main, mu_lora=mu_lora,
                  w0=_row(rw_w0[0]), w2=_pad_rows(rw_w2[0], LORA_PAD).astype(BF16),
                  a0=_row(rw_a0[0]), a2=_pad_rows(rw_a2[0], LORA_PAD).astype(BF16),
                  g2=rw_g2[0].astype(BF16), kk=_row(rw_kk[0]), ka=_row(rw_ka[0]), rk=_row(rw_rk[0]),
                  ln_g=_row(rw_ln_g[0]), ln_b=_row(rw_ln_b[0]), qn=tile2(sb_qn[0]), kn=tile2(sb_kn[0]))
    w_odd = dict(common(1), w_in=_odd_in_weight(w_in_odd[0]), conv_w=m_conv_w[0], conv_b=_row(m_conv_b[0]),
                 dtb_row=_pad_cols(_row(m_dt_bias[0]), LANE), dtb_col=m_dt_bias[0][:, None],
                 alog_row=_pad_cols(_row(m_a_log[0]), LANE), alog_col=m_a_log[0][:, None],
                 d_exp=_row(jnp.repeat(m_d[0], HEAD_DIM)), norm_g=_row(m_norm_g[0]),
                 bf=_pad_cols(_row(fox_bf[0]), LANE), qn=tile2(fox_qn[0]), kn=tile2(fox_kn[0]))

    st_even = dict(shift=state_rwkv_shift[0], rwkv=state_rwkv[0],
                   sb_k=cache_sb_k[0].reshape(n_pool, PAGE, HW), sb_v=cache_sb_v[0].reshape(n_pool, PAGE, HW))
    st_odd = dict(ssm=state_ssm[0], conv=state_conv[0],
                  fox_k=cache_fox_k[0].reshape(n_pool, PAGE, HW), fox_v=cache_fox_v[0].reshape(n_pool, PAGE, HW),
                  fox_lf_t=jnp.swapaxes(cache_fox_logf[0], 1, 2))

    xp, ep = _even_layer(x_prompt[0], mod_p(0), w_even, None, None)
    xp, op = _odd_layer(xp, mod_p(1), w_odd, None, None)
    xs, es = _even_layer(x_sample[:, 0], mod_s(0), w_even, st_even, page_table)
    xs, os_ = _odd_layer(xs, mod_s(1), w_odd, st_odd, page_table)

    seq = lambda a: a.reshape((1, 1, t) + a.shape[1:])
    tok = lambda a: a.reshape((1, bs, 1) + a.shape[1:])
    return (xp[None], xs[:, None, :],
            seq(_heads(ep["k"])), tok(_heads(es["k"])), seq(_heads(ep["v"])), tok(_heads(es["v"])),
            ep["rwkv"][None], es["rwkv"][None], ep["shift"][None], es["shift"][None],
            seq(_heads(op["k"])), tok(_heads(os_["k"])), seq(_heads(op["v"])), tok(_heads(os_["v"])),
            seq(op["logf"]), tok(os_["logf"]),
            op["ssm"][None, None], os_["ssm"][None], op["conv"][None, None], os_["conv"][None])
```

```python
import functools

import jax
import jax.numpy as jnp
from jax import lax
from jax.experimental import pallas as pl
from jax.experimental.pallas import tpu as pltpu

F32 = jnp.float32
BF16 = jnp.bfloat16
HIGHEST = lax.Precision.HIGHEST

HEAD_DIM = 64
N_HEADS = 16
HW = N_HEADS * HEAD_DIM
LANE = 128
NORM_EPS = 1e-6
GN_EPS = 64e-5
LORA_PAD = 128
LORA_GATE = 256
SSM_GROUPS = 4
SSM_STATE = 128
CONV_W = 4
PAGE = 128
SUB = 16
TILE = 128
V7X_VMEM_LIMIT = 48 * 1024 * 1024
F32_EXP_UNDERFLOW = 105.0

EV_R, EV_K, EV_V = 0, HW, 2 * HW
EV_SBQ, EV_SBK, EV_SBV = 3 * HW, 4 * HW, 5 * HW
EV_LORA = 6 * HW
LORA_W = 2 * LORA_PAD + LORA_GATE
RW_MAIN = 3 * HW
OD_XBC, OD_Z = 0, 2 * HW
OD_Q, OD_K, OD_V = 3 * HW, 4 * HW, 5 * HW
OD_DT, OD_F = 6 * HW, 6 * HW + LANE
PROJ_PAD = 6 * HW + LORA_W


def _cp(sems, vmem=V7X_VMEM_LIMIT):
    return pltpu.CompilerParams(dimension_semantics=sems, vmem_limit_bytes=vmem)


def _dot(a, b):
    return jnp.dot(a, b, preferred_element_type=F32)


def _dot_hi(a, b):
    return jnp.dot(a, b, preferred_element_type=F32, precision=HIGHEST)


def _dot_nt(a, b, precision=None):
    return lax.dot_general(a, b, (((1,), (1,)), ((), ())), preferred_element_type=F32,
                           precision=precision)


def _split3(x):
    hi = x.astype(BF16)
    r1 = x - hi.astype(F32)
    mid = r1.astype(BF16)
    lo = (r1 - mid.astype(F32)).astype(BF16)
    return hi, mid, lo


def _dot_x01(x, m):
    hi, mid, lo = _split3(x)
    return _dot(hi, m) + _dot(mid, m) + _dot(lo, m)


def _dot_01x(m, x):
    hi, mid, lo = _split3(x)
    return _dot(m, hi) + _dot(m, mid) + _dot(m, lo)


def _sigmoid(x):
    return jax.nn.sigmoid(x)


def _softplus(x):
    return jnp.maximum(x, 0.0) + jnp.log(1.0 + jnp.exp(-jnp.abs(x)))


def _iota2(shape, dim):
    return lax.broadcasted_iota(jnp.int32, shape, dim)


def _block_ones(n, seg):
    i, j = _iota2((n, n), 0), _iota2((n, n), 1)
    return jnp.where((i // seg) == (j // seg), 1.0, 0.0).astype(BF16)


def _ada_kernel(c_ref, w_ref, b_ref, o_ref):
    c = c_ref[...]
    a = (c * _sigmoid(c)).astype(BF16)
    o_ref[...] = _dot(a, w_ref[...].astype(BF16)) + b_ref[...]


def ada_mod(c_all, w_ada, b_ada, tn=1024):
    depth, d, n = w_ada.shape
    mp = c_all.shape[0]
    return pl.pallas_call(
        _ada_kernel,
        out_shape=jax.ShapeDtypeStruct((depth, mp, n), F32),
        grid=(depth, n // tn),
        in_specs=[pl.BlockSpec((mp, d), lambda l, j: (0, 0)),
                  pl.BlockSpec((None, d, tn), lambda l, j: (l, 0, j)),
                  pl.BlockSpec((None, 1, tn), lambda l, j: (l, 0, j))],
        out_specs=pl.BlockSpec((None, mp, tn), lambda l, j: (l, 0, j)),
        compiler_params=_cp(("parallel", "parallel")),
        name="ada_mod",
    )(c_all, w_ada, b_ada.reshape(depth, 1, n))


def _modulate(x, g, sh, sc):
    y = x * lax.rsqrt(jnp.mean(x * x, axis=-1, keepdims=True) + NORM_EPS)
    return y * g * (1.0 + sc) + sh


def _inproj_kernel(x_ref, g_ref, sh_ref, sc_ref, w_ref, o_ref, h_scr):
    @pl.when(pl.program_id(1) == 0)
    def _():
        h_scr[...] = _modulate(x_ref[...], g_ref[...], sh_ref[...], sc_ref[...]).astype(BF16)

    o_ref[...] = _dot(h_scr[...], w_ref[...])


def _row_spec(arr, tm, width):
    if arr.shape[0] == 1:
        return pl.BlockSpec((1, width), lambda i, j: (0, 0))
    return pl.BlockSpec((tm, width), lambda i, j: (i, 0))


def in_proj(x, g, sh, sc, w, tn=512):
    m, d = x.shape
    n = w.shape[1]
    tm = min(m, 512)
    return pl.pallas_call(
        _inproj_kernel,
        out_shape=jax.ShapeDtypeStruct((m, n), F32),
        grid=(m // tm, n // tn),
        in_specs=[pl.BlockSpec((tm, d), lambda i, j: (i, 0)),
                  pl.BlockSpec((1, d), lambda i, j: (0, 0)),
                  _row_spec(sh, tm, d), _row_spec(sc, tm, d),
                  pl.BlockSpec((d, tn), lambda i, j: (0, j))],
        out_specs=pl.BlockSpec((tm, tn), lambda i, j: (i, j)),
        scratch_shapes=[pltpu.VMEM((tm, d), BF16)],
        compiler_params=_cp(("parallel", "arbitrary")),
        name="in_proj",
    )(x, g, sh, sc, w)


def _outproj_kernel(a_ref, b_ref, wa_ref, wb_ref, x_ref, gt_ref, o_ref):
    acc = _dot(a_ref[...], wa_ref[...]) + _dot(b_ref[...], wb_ref[...])
    o_ref[...] = x_ref[...] + gt_ref[...] * acc


def out_proj(a, b, wa, wb, x, gt, tn=1024):
    m, ka = a.shape
    n = wa.shape[1]
    tm = min(m, 512)
    gt_spec = (pl.BlockSpec((1, tn), lambda i, j: (0, j)) if gt.shape[0] == 1
               else pl.BlockSpec((tm, tn), lambda i, j: (i, j)))
    return pl.pallas_call(
        _outproj_kernel,
        out_shape=jax.ShapeDtypeStruct((m, n), F32),
        grid=(m // tm, n // tn),
        in_specs=[pl.BlockSpec((tm, ka), lambda i, j: (i, 0)),
                  pl.BlockSpec((tm, ka), lambda i, j: (i, 0)),
                  pl.BlockSpec((ka, tn), lambda i, j: (0, j)),
                  pl.BlockSpec((ka, tn), lambda i, j: (0, j)),
                  pl.BlockSpec((tm, tn), lambda i, j: (i, j)),
                  gt_spec],
        out_specs=pl.BlockSpec((tm, tn), lambda i, j: (i, j)),
        compiler_params=_cp(("parallel", "parallel")),
        name="out_proj",
    )(a, b, wa, wb, x, gt)


def _ffn_kernel(x_ref, g_ref, sh_ref, sc_ref, gt_ref, wg_ref, wu_ref, wd_ref, o_ref, h_scr, acc_scr):
    f = pl.program_id(1)

    @pl.when(f == 0)
    def _():
        h_scr[...] = _modulate(x_ref[...], g_ref[...], sh_ref[...], sc_ref[...]).astype(BF16)
        acc_scr[...] = jnp.zeros_like(acc_scr)

    h = h_scr[...]
    gate = _dot(h, wg_ref[...])
    up = _dot(h, wu_ref[...])
    act = (gate * _sigmoid(gate) * up).astype(BF16)
    acc_scr[...] += _dot(act, wd_ref[...])

    @pl.when(f == pl.num_programs(1) - 1)
    def _():
        o_ref[...] = x_ref[...] + gt_ref[...] * acc_scr[...]


def ffn(x, g, sh, sc, gt, wg, wu, wd, tf=512):
    m, d = x.shape
    dff = wg.shape[1]
    tm = min(m, 512)
    return pl.pallas_call(
        _ffn_kernel,
        out_shape=jax.ShapeDtypeStruct((m, d), F32),
        grid=(m // tm, dff // tf),
        in_specs=[pl.BlockSpec((tm, d), lambda i, f: (i, 0)),
                  pl.BlockSpec((1, d), lambda i, f: (0, 0)),
                  _row_spec(sh, tm, d), _row_spec(sc, tm, d), _row_spec(gt, tm, d),
                  pl.BlockSpec((d, tf), lambda i, f: (0, f)),
                  pl.BlockSpec((d, tf), lambda i, f: (0, f)),
                  pl.BlockSpec((tf, d), lambda i, f: (f, 0))],
        out_specs=pl.BlockSpec((tm, d), lambda i, f: (i, 0)),
        scratch_shapes=[pltpu.VMEM((tm, d), BF16), pltpu.VMEM((tm, d), F32)],
        compiler_params=_cp(("parallel", "arbitrary")),
        name="ffn",
    )(x, g, sh, sc, gt, wg, wu, wd)


def _rwkv_prep_body(pm, pl_, ppm, ppl, prow, mum_ref, mul_ref, w0_ref, w2_ref, a0_ref, a2_ref, g2_ref, kk_ref,
                    ka_ref, rk_ref, outs):
    r_o, k_o, v_o, kap_o, a_o, lw_o, g_o, bon_o = outs
    tt = pm.shape[0]

    def lerp(cur, prev, mu, off, width):
        sl = slice(off, off + width)
        return cur[:, sl] + (prev[prow:prow + tt, sl] - cur[:, sl]) * mu[:, sl]

    xs = lambda off, width: lerp(pm, ppm, mum_ref, off, width)
    twd = jnp.tanh(lerp(pl_, ppl, mul_ref, 0, LORA_PAD)).astype(BF16)
    xad = lerp(pl_, ppl, mul_ref, LORA_PAD, LORA_PAD).astype(BF16)
    sgd = _sigmoid(lerp(pl_, ppl, mul_ref, 2 * LORA_PAD, LORA_GATE)).astype(BF16)
    bo = _block_ones(LANE, HEAD_DIM)
    for c in range(HW // LANE):
        cs = slice(c * LANE, (c + 1) * LANE)
        r = xs(EV_R + c * LANE, LANE)
        k = xs(EV_K + c * LANE, LANE)
        v = xs(EV_V + c * LANE, LANE)
        w_log = -_softplus(-(w0_ref[:, cs] + _dot(twd, w2_ref[:, cs]))) - 0.5
        a = _sigmoid(a0_ref[:, cs] + _dot(xad, a2_ref[:, cs]))
        kkv = k * kk_ref[:, cs]
        kap = kkv * lax.rsqrt(jnp.maximum(_dot_x01(kkv * kkv, bo), 1e-24))
        kmod = k * (1.0 + (a - 1.0) * ka_ref[:, cs])
        r_o[:, cs] = r
        k_o[:, cs] = kmod
        v_o[:, cs] = v
        kap_o[:, cs] = kap
        a_o[:, cs] = a
        lw_o[:, cs] = -jnp.exp(w_log)
        g_o[:, cs] = _dot(sgd, g2_ref[:, cs])
        bon_o[:, cs] = _dot_x01(r * kmod * rk_ref[:, cs], bo) * v


N_PREP_PARAMS = 10
N_PREP_OUTS = 8


def _rwkv_prep_seq_kernel(pm_ref, pl_ref, s0m_ref, s0l_ref, *rest):
    params = rest[:N_PREP_PARAMS]
    outs = rest[N_PREP_PARAMS:N_PREP_PARAMS + N_PREP_OUTS]
    extm, extl = rest[N_PREP_PARAMS + N_PREP_OUTS:]
    tt = pm_ref.shape[0]

    @pl.when(pl.program_id(0) == 0)
    def _():
        extm[0:8, :] = jnp.broadcast_to(s0m_ref[...], (8, extm.shape[1]))
        extl[0:8, :] = jnp.broadcast_to(s0l_ref[...], (8, extl.shape[1]))

    extm[8:8 + tt, :] = pm_ref[...]
    extl[8:8 + tt, :] = pl_ref[...]
    _rwkv_prep_body(pm_ref, pl_ref, extm, extl, 7, *params, outs)
    extm[0:8, :] = extm[tt:tt + 8, :]
    extl[0:8, :] = extl[tt:tt + 8, :]


def _rwkv_prep_step_kernel(pm_ref, pl_ref, prevm_ref, prevl_ref, *rest):
    params = rest[:N_PREP_PARAMS]
    outs = rest[N_PREP_PARAMS:N_PREP_PARAMS + N_PREP_OUTS]
    _rwkv_prep_body(pm_ref, pl_ref, prevm_ref, prevl_ref, 0, *params, outs)


def rwkv_prep(p, prev_main, prev_lora, prm, sequential, tt=256):
    m = p.shape[0]
    tt = min(tt, m)
    full = lambda a: pl.BlockSpec(a.shape, lambda i: (0, 0))
    params = [prm[n] for n in ("mu_main", "mu_lora", "w0", "w2", "a0", "a2", "g2", "kk", "ka", "rk")]
    assert len(params) == N_PREP_PARAMS
    pm_spec = pl.BlockSpec((tt, RW_MAIN), lambda i: (i, 0))
    pl_spec = pl.BlockSpec((tt, LORA_W), lambda i: (i, EV_LORA // LORA_W))
    if sequential:
        kern = _rwkv_prep_seq_kernel
        prev_specs = [pl.BlockSpec((1, RW_MAIN), lambda i: (0, 0)), pl.BlockSpec((1, LORA_W), lambda i: (0, 0))]
        scratch = [pltpu.VMEM((tt + 8, RW_MAIN), F32), pltpu.VMEM((tt + 8, LORA_W), F32)]
    else:
        kern = _rwkv_prep_step_kernel
        prev_specs = [pl.BlockSpec((tt, RW_MAIN), lambda i: (i, 0)), pl.BlockSpec((tt, LORA_W), lambda i: (i, 0))]
        scratch = []
    out_spec = pl.BlockSpec((tt, HW), lambda i: (i, 0))
    return pl.pallas_call(
        kern,
        out_shape=[jax.ShapeDtypeStruct((m, HW), F32)] * N_PREP_OUTS,
        grid=(m // tt,),
        in_specs=[pm_spec, pl_spec] + prev_specs + [full(a) for a in params],
        out_specs=[out_spec] * N_PREP_OUTS,
        scratch_shapes=scratch,
        compiler_params=_cp(("arbitrary",)),
        name="rwkv_prep",
    )(p, p, prev_main, prev_lora, *params)


def _rwkv_head_tile(r, k, v, kap, a, lw, h0):
    n = TILE
    row, col = _iota2((n, n), 0), _iota2((n, n), 1)
    same = (row // SUB) == (col // SUB)
    m_incl = jnp.where(same & (row >= col), 1.0, 0.0).astype(BF16)
    m_all = jnp.where(same, 1.0, 0.0).astype(BF16)
    strict = same & (row > col)
    incl = same & (row >= col)
    cum = _dot_01x(m_incl, lw)
    cend = _dot_01x(m_all, lw)
    e_pos, e_neg, e_end = jnp.exp(cum), jnp.exp(-cum), jnp.exp(cend - cum)
    b = kap * a
    kt, rt = kap * jnp.exp(cum - lw), r * e_pos
    bt, ktil = b * e_neg, k * e_neg
    bh, kh = b * e_end, k * e_end
    pend = jnp.exp(cend)
    qk = _dot_nt(jnp.concatenate([kt, rt], axis=0), jnp.concatenate([bt, ktil], axis=0), HIGHEST)
    zero = jnp.zeros((n, n), F32)
    neg_a = jnp.where(strict, -qk[:n, :n], zero)
    bm = jnp.where(strict, qk[:n, n:], zero)
    mb = jnp.where(incl, qk[n:, :n], zero)
    mk = jnp.where(incl, qk[n:, n:], zero)
    n2 = _dot_hi(neg_a, neg_a)
    n4 = _dot_hi(n2, n2)
    n8 = _dot_hi(n4, n4)
    tm = jnp.where(row == col, 1.0, 0.0) + neg_a
    tm = tm + _dot_hi(tm, n2)
    tm = tm + _dot_hi(tm, n4)
    tm = tm + _dot_hi(tm, n8)
    w = _dot_hi(tm, kt)
    ub = _dot_hi(tm, _dot_hi(bm, v))
    rp = rt - _dot_hi(mb, w)
    op = _dot_hi(mk, v) - _dot_hi(mb, ub)
    bkt = jnp.concatenate([bh, kh], axis=1).T
    bht, kht = bkt[:HEAD_DIM, :], bkt[HEAD_DIM:, :]
    lane_t = _iota2((HEAD_DIM, n), 1)
    eye = _iota2((HEAD_DIM, HEAD_DIM), 0) == _iota2((HEAD_DIM, HEAD_DIM), 1)
    nsub = n // SUB
    masks = [(lane_t // SUB) == c for c in range(nsub)]
    stack_b = jnp.concatenate([jnp.where(mk_, bht, 0.0) for mk_ in masks], axis=0)
    stack_k = jnp.concatenate([jnp.where(mk_, kht, 0.0) for mk_ in masks], axis=0)
    gg = _dot_hi(stack_b, w)
    nn = _dot_hi(jnp.concatenate([stack_k, stack_b], axis=1), jnp.concatenate([v, -ub], axis=0))
    h = h0
    outs = []
    for c in range(nsub):
        blk = slice(c * HEAD_DIM, (c + 1) * HEAD_DIM)
        rows = slice(c * SUB, (c + 1) * SUB)
        g_c = jnp.where(eye, pend[c * SUB:c * SUB + 1, :], 0.0) - gg[blk]
        x = jnp.concatenate([g_c, rp[rows]], axis=0)
        x_hi = x.astype(BF16)
        x_lo = (x - x_hi.astype(F32)).astype(BF16)
        h_hi = h.astype(BF16)
        h_lo = (h - h_hi.astype(F32)).astype(BF16)
        res = _dot(jnp.concatenate([x_hi, x_lo, x_hi], axis=1), jnp.concatenate([h_hi, h_hi, h_lo], axis=0))
        outs.append(res[HEAD_DIM:] + op[rows])
        h = res[:HEAD_DIM] + nn[blk]
    return jnp.concatenate(outs, axis=0), h


def _rwkv_scan_kernel(r_ref, k_ref, v_ref, kap_ref, a_ref, lw_ref, g_ref, bon_ref, lng_ref, lnb_ref,
                      o_ref, hfin_ref, h_scr):
    @pl.when(pl.program_id(1) == 0)
    def _():
        h_scr[...] = jnp.zeros_like(h_scr)

    res = []
    for hh in range(2):
        sl = slice(hh * HEAD_DIM, (hh + 1) * HEAD_DIM)
        o, h = _rwkv_head_tile(r_ref[:, sl], k_ref[:, sl], v_ref[:, sl], kap_ref[:, sl], a_ref[:, sl],
                               lw_ref[:, sl], h_scr[hh])
        h_scr[hh] = h
        hfin_ref[hh] = h
        mean = jnp.mean(o, axis=-1, keepdims=True)
        var = jnp.mean(jnp.square(o - mean), axis=-1, keepdims=True)
        res.append((o - mean) * lax.rsqrt(var + GN_EPS))
    on = jnp.concatenate(res, axis=1) * lng_ref[...] + lnb_ref[...]
    o_ref[...] = ((on + bon_ref[...]) * g_ref[...]).astype(o_ref.dtype)


def rwkv_scan(r, k, v, kap, a, lw, g, bon, lng, lnb):
    t = r.shape[0]
    tile = pl.BlockSpec((TILE, LANE), lambda p, i: (i, p))
    vec = pl.BlockSpec((1, LANE), lambda p, i: (0, p))
    return pl.pallas_call(
        _rwkv_scan_kernel,
        out_shape=[jax.ShapeDtypeStruct((t, HW), BF16),
                   jax.ShapeDtypeStruct((N_HEADS, HEAD_DIM, HEAD_DIM), F32)],
        grid=(HW // LANE, t // TILE),
        in_specs=[tile] * 8 + [vec, vec],
        out_specs=[tile, pl.BlockSpec((2, HEAD_DIM, HEAD_DIM), lambda p, i: (p, 0, 0))],
        scratch_shapes=[pltpu.VMEM((2, HEAD_DIM, HEAD_DIM), F32)],
        compiler_params=_cp(("parallel", "arbitrary")),
        name="rwkv_scan",
    )(r, k, v, kap, a, lw, g, bon, lng, lnb)


def _col_bcast(rows, width):
    nr = rows.shape[0]
    eye = _iota2((HEAD_DIM, HEAD_DIM), 0) == _iota2((HEAD_DIM, HEAD_DIM), 1)
    diag = jnp.concatenate([jnp.where(eye, rows[j:j + 1, :], 0.0) for j in range(nr)], axis=0)
    return _dot_x01(diag, jnp.ones((HEAD_DIM, width), BF16))


def _rwkv_step_kernel(s_ref, r_ref, k_ref, v_ref, kap_ref, a_ref, lw_ref, g_ref, bon_ref, lng_ref, lnb_ref,
                      o_ref, so_ref):
    nb = s_ref.shape[0]
    bo = _block_ones(LANE, HEAD_DIM)
    for bi in range(nb):
        pieces = []
        for h in range(N_HEADS):
            sl = slice(h * HEAD_DIM, (h + 1) * HEAD_DIM)
            row = lambda ref: ref[bi:bi + 1, sl]
            s = s_ref[bi, h]
            kap, w = row(kap_ref), jnp.exp(row(lw_ref))
            sa = jnp.sum(s * kap, axis=-1, keepdims=True)
            vcol = _col_bcast(row(v_ref), HEAD_DIM)
            s_new = s * w - sa * (kap * row(a_ref)) + vcol * row(k_ref)
            so_ref[bi, h] = s_new
            r8 = jnp.broadcast_to(row(r_ref), (8, HEAD_DIM))
            pieces.append(_dot_nt(r8, s_new, HIGHEST)[0:1, :])
        o = jnp.concatenate(pieces, axis=1)
        res = []
        for c in range(HW // LANE):
            oc = o[:, c * LANE:(c + 1) * LANE]
            o8 = jnp.broadcast_to(oc, (8, LANE))
            mean = _dot_x01(o8, bo) * (1.0 / HEAD_DIM)
            cen = o8 - mean
            var = _dot_x01(cen * cen, bo) * (1.0 / HEAD_DIM)
            res.append((cen * lax.rsqrt(var + GN_EPS))[0:1, :])
        on = jnp.concatenate(res, axis=1) * lng_ref[...] + lnb_ref[...]
        o_ref[bi:bi + 1, :] = ((on + bon_ref[bi:bi + 1, :]) * g_ref[bi:bi + 1, :]).astype(o_ref.dtype)


def rwkv_step(s0, r, k, v, kap, a, lw, g, bon, lng, lnb, nb=8):
    b = s0.shape[0]
    st = pl.BlockSpec((nb, N_HEADS, HEAD_DIM, HEAD_DIM), lambda i: (i, 0, 0, 0))
    rows = pl.BlockSpec((nb, HW), lambda i: (i, 0))
    vec = pl.BlockSpec((1, HW), lambda i: (0, 0))
    return pl.pallas_call(
        _rwkv_step_kernel,
        out_shape=[jax.ShapeDtypeStruct((b, HW), F32), jax.ShapeDtypeStruct(s0.shape, F32)],
        grid=(b // nb,),
        in_specs=[st] + [rows] * 8 + [vec, vec],
        out_specs=[rows, st],
        compiler_params=_cp(("parallel",)),
        name="rwkv_step",
    )(s0, r, k, v, kap, a, lw, g, bon, lng, lnb)


def _head_rmsnorm(x, gain, bo):
    ms = _dot_x01(x * x, bo) * (1.0 / HEAD_DIM)
    return x * lax.rsqrt(ms + NORM_EPS) * gain


def _qknorm_kernel(q_ref, k_ref, qg_ref, kg_ref, qo_ref, ko_ref):
    bo = _block_ones(LANE, HEAD_DIM)
    for c in range(HW // LANE):
        cs = slice(c * LANE, (c + 1) * LANE)
        qo_ref[:, cs] = (_head_rmsnorm(q_ref[:, cs], qg_ref[...], bo) * (HEAD_DIM ** -0.5)).astype(qo_ref.dtype)
        ko_ref[:, cs] = _head_rmsnorm(k_ref[:, cs], kg_ref[...], bo)


def qk_norm(p, q_off, qg, kg, tt=256):
    m = p.shape[0]
    tt = min(tt, m)
    qb, kb = q_off // HW, q_off // HW + 1
    assert q_off % HW == 0
    gain = pl.BlockSpec((1, LANE), lambda i: (0, 0))
    out = pl.BlockSpec((tt, HW), lambda i: (i, 0))
    return pl.pallas_call(
        _qknorm_kernel,
        out_shape=[jax.ShapeDtypeStruct((m, HW), BF16), jax.ShapeDtypeStruct((m, HW), F32)],
        grid=(m // tt,),
        in_specs=[pl.BlockSpec((tt, HW), lambda i: (i, qb)), pl.BlockSpec((tt, HW), lambda i: (i, kb)),
                  gain, gain],
        out_specs=[out, out],
        compiler_params=_cp(("parallel",)),
        name="qk_norm",
    )(p, p, qg, kg)


def _logf_kernel(f_ref, bf_ref, lf_ref, fc_ref, carry):
    tt = f_ref.shape[0]

    @pl.when(pl.program_id(0) == 0)
    def _():
        carry[...] = jnp.zeros_like(carry)

    lf = -_softplus(-(f_ref[...] + bf_ref[...]))
    lf_ref[...] = lf
    lower = jnp.where(_iota2((tt, tt), 0) >= _iota2((tt, tt), 1), 1.0, 0.0).astype(BF16)
    fc = _dot_01x(lower, lf) + carry[0:1, :]
    fc_ref[...] = fc
    carry[...] = jnp.broadcast_to(fc[tt - 1:tt, :], carry.shape)


def logf_cumsum(p, f_off, bf_pad, tt=128):
    m = p.shape[0]
    blk = pl.BlockSpec((tt, LANE), lambda i: (i, 0))
    return pl.pallas_call(
        _logf_kernel,
        out_shape=[jax.ShapeDtypeStruct((m, LANE), F32)] * 2,
        grid=(m // tt,),
        in_specs=[pl.BlockSpec((tt, LANE), lambda i: (i, f_off // LANE)),
                  pl.BlockSpec((1, LANE), lambda i: (0, 0))],
        out_specs=[blk, blk],
        scratch_shapes=[pltpu.VMEM((8, LANE), F32)],
        compiler_params=_cp(("arbitrary",)),
        name="logf_cumsum",
    )(p, bf_pad)


SB_TQ = 256
SB_TK = 128


def _sb_block(q, kb, vb, carry, u2, mask):
    z = _dot_nt(q, kb)
    lk = -_softplus(z)
    if mask is not None:
        lk = jnp.where(mask, lk, 0.0)
    cs = _dot_x01(lk, u2)
    tk = kb.shape[0]
    wgt = jnp.exp(lk + z + cs[:, :tk] + carry)
    if mask is not None:
        wgt = jnp.where(mask, wgt, 0.0)
    return _dot(wgt.astype(BF16), vb), carry + cs[:, tk:]


def _sb_attn_kernel(q_ref, k_ref, v_ref, o_ref):
    i = pl.program_id(1)
    tq, tk = SB_TQ, SB_TK
    r, c = _iota2((tk, 2 * tk), 0), _iota2((tk, 2 * tk), 1)
    u2 = jnp.where((c >= tk) | (r > c), 1.0, 0.0).astype(BF16)
    ratio = tq // tk
    outs = []
    for hh in range(2):
        sl = slice(hh * HEAD_DIM, (hh + 1) * HEAD_DIM)
        q = q_ref[:, sl]
        acc = jnp.zeros((tq, HEAD_DIM), F32)
        carry = jnp.zeros((tq, tk), F32)
        for d in range(ratio):
            start = pl.multiple_of((i * ratio + (ratio - 1 - d)) * tk, tk)
            kb = k_ref[pl.ds(start, tk), sl].astype(BF16)
            vb = v_ref[pl.ds(start, tk), sl].astype(BF16)
            qpos = _iota2((tq, tk), 0)
            kpos = _iota2((tq, tk), 1) + (ratio - 1 - d) * tk
            part, carry = _sb_block(q, kb, vb, carry, u2, kpos < qpos)
            acc = acc + part

        def live(carry):
            return jnp.max(carry) > -F32_EXP_UNDERFLOW

        def cond(st):
            return (st[0] >= 0) & st[3]

        def body(st):
            j, acc, carry, _ = st
            start = pl.multiple_of(j * tk, tk)
            kb = k_ref[pl.ds(start, tk), sl].astype(BF16)
            vb = v_ref[pl.ds(start, tk), sl].astype(BF16)
            part, carry = _sb_block(q, kb, vb, carry, u2, None)
            return j - 1, acc + part, carry, live(carry)

        _, acc, _, _ = lax.while_loop(cond, body, (i * ratio - 1, acc, carry, live(carry)))
        outs.append(acc)
    o_ref[...] = jnp.concatenate(outs, axis=1).astype(o_ref.dtype)


def sb_attention(q, k, v):
    t = q.shape[0]
    seq = pl.BlockSpec((t, LANE), lambda p, i: (0, p))
    blk = pl.BlockSpec((SB_TQ, LANE), lambda p, i: (i, p))
    return pl.pallas_call(
        _sb_attn_kernel,
        out_shape=jax.ShapeDtypeStruct((t, HW), BF16),
        grid=(HW // LANE, t // SB_TQ),
        in_specs=[blk, seq, seq],
        out_specs=blk,
        compiler_params=_cp(("parallel", "arbitrary")),
        name="sb_attention",
    )(q, k, v)


FOX_TQ = 256
FOX_TK = 128


def _fox_block(q, kb, vb, fq, fk, m, l, acc, mask):
    s = _dot_nt(q, kb) + (fq - fk)
    if mask is not None:
        s = jnp.where(mask, s, -jnp.inf)
    m_new = jnp.maximum(m, jnp.max(s, axis=-1, keepdims=True))
    alpha = jnp.exp(m - m_new)
    pr = jnp.exp(s - m_new)
    l = alpha * l + jnp.sum(pr, axis=-1, keepdims=True)
    acc = alpha * acc + _dot(pr.astype(BF16), vb)
    return m_new, l, acc


def _fox_attn_kernel(q_ref, k_ref, v_ref, fq_ref, fk_ref, qg_ref, kg_ref, o_ref):
    i = pl.program_id(1)
    tq, tk = FOX_TQ, FOX_TK
    ratio = tq // tk
    dot_bound = 1.02 * HEAD_DIM ** 0.5 * jnp.max(jnp.abs(qg_ref[...])) * jnp.max(jnp.abs(kg_ref[...]))
    outs = []
    for hh in range(2):
        sl = slice(hh * HEAD_DIM, (hh + 1) * HEAD_DIM)
        q = q_ref[:, sl]
        fq = fq_ref[hh]
        m = jnp.full((tq, 1), -jnp.inf, F32)
        l = jnp.zeros((tq, 1), F32)
        acc = jnp.zeros((tq, HEAD_DIM), F32)
        for d in range(ratio):
            j = i * ratio + d
            start = pl.multiple_of(j * tk, tk)
            kb = k_ref[pl.ds(start, tk), sl].astype(BF16)
            vb = v_ref[pl.ds(start, tk), sl].astype(BF16)
            fk = fk_ref[hh, pl.ds(j, 1), :]
            qpos = _iota2((tq, tk), 0)
            kpos = _iota2((tq, tk), 1) + d * tk
            m, l, acc = _fox_block(q, kb, vb, fq, fk, m, l, acc, kpos <= qpos)

        def live(j, m):
            fk_min = jnp.min(fk_ref[hh, pl.ds(jnp.maximum(j, 0), 1), :], axis=-1, keepdims=True)
            return jnp.max(dot_bound + fq - fk_min - m) > -F32_EXP_UNDERFLOW

        def cond(st):
            return (st[0] >= 0) & st[4]

        def body(st):
            j, m, l, acc, _ = st
            start = pl.multiple_of(j * tk, tk)
            kb = k_ref[pl.ds(start, tk), sl].astype(BF16)
            vb = v_ref[pl.ds(start, tk), sl].astype(BF16)
            fk = fk_ref[hh, pl.ds(j, 1), :]
            m, l, acc = _fox_block(q, kb, vb, fq, fk, m, l, acc, None)
            return j - 1, m, l, acc, live(j - 1, m)

        j0 = i * ratio - 1
        _, m, l, acc, _ = lax.while_loop(cond, body, (j0, m, l, acc, live(j0, m)))
        outs.append(acc / l)
    o_ref[...] = jnp.concatenate(outs, axis=1).astype(o_ref.dtype)


def fox_attention(q, k, v, f_col, f_row, qg, kg):
    t = q.shape[0]
    seq = pl.BlockSpec((t, LANE), lambda p, i: (0, p))
    blk = pl.BlockSpec((FOX_TQ, LANE), lambda p, i: (i, p))
    gain = pl.BlockSpec((1, LANE), lambda p, i: (0, 0))
    return pl.pallas_call(
        _fox_attn_kernel,
        out_shape=jax.ShapeDtypeStruct((t, HW), BF16),
        grid=(HW // LANE, t // FOX_TQ),
        in_specs=[blk, seq, seq,
                  pl.BlockSpec((2, FOX_TQ, 1), lambda p, i: (p, i, 0)),
                  pl.BlockSpec((2, t // FOX_TK, FOX_TK), lambda p, i: (p, 0, 0)), gain, gain],
        out_specs=blk,
        compiler_params=_cp(("parallel", "arbitrary")),
        name="fox_attention",
    )(q, k, v, f_col, f_row, qg, kg)


PAGES_PER_STEP = 4


def _suffix_sum(x):
    n = x.shape[-1]
    lane = _iota2(x.shape, x.ndim - 1)
    y, d = x, 1
    while d < n:
        y = y + jnp.where(lane + d < n, pltpu.roll(y, n - d, axis=x.ndim - 1), 0.0)
        d *= 2
    return y


def _head_diag(acc):
    keep = (_iota2(acc.shape, 1) // HEAD_DIM) == _iota2(acc.shape, 0)
    return jnp.sum(jnp.where(keep, acc, 0.0), axis=0, keepdims=True)


def _sb_decode_kernel(pt_ref, qm_ref, *rest):
    del pt_ref
    n = PAGES_PER_STEP
    k_refs, v_refs, o_ref, acc, carry = rest[:n], rest[n:2 * n], rest[2 * n], rest[2 * n + 1], rest[2 * n + 2]
    g = pl.program_id(1)

    @pl.when(g == 0)
    def _():
        acc[...] = jnp.zeros_like(acc)
        carry[...] = jnp.zeros_like(carry)

    qm = qm_ref[...]
    for s in range(n):
        z = _dot_nt(qm, k_refs[s][...].astype(BF16))
        lk = -_softplus(z)
        suf = _suffix_sum(lk)
        wgt = jnp.exp(z + suf + carry[...])
        acc[...] += _dot(wgt.astype(BF16), v_refs[s][...].astype(BF16))
        carry[...] += suf[:, 0:1]

    @pl.when(g == pl.num_programs(1) - 1)
    def _():
        o_ref[...] = _head_diag(acc[...])


def _page_specs(n_pages):
    n = PAGES_PER_STEP
    specs = []
    for s in range(n):
        specs.append(pl.BlockSpec(
            (None, PAGE, HW), lambda b, g, pt, s=s: (pt[b, n_pages - 1 - (g * n + s)], 0, 0)))
    return specs


def sb_decode(qm, cache_k, cache_v, page_table):
    b, n_pages = page_table.shape
    n = PAGES_PER_STEP
    grid_spec = pltpu.PrefetchScalarGridSpec(
        num_scalar_prefetch=1,
        grid=(b, n_pages // n),
        in_specs=[pl.BlockSpec((None, N_HEADS, HW), lambda bb, g, pt: (bb, 0, 0))]
                 + _page_specs(n_pages) + _page_specs(n_pages),
        out_specs=pl.BlockSpec((None, 1, HW), lambda bb, g, pt: (bb, 0, 0)),
        scratch_shapes=[pltpu.VMEM((N_HEADS, HW), F32), pltpu.VMEM((N_HEADS, 1), F32)],
    )
    return pl.pallas_call(
        _sb_decode_kernel,
        out_shape=jax.ShapeDtypeStruct((b, 1, HW), F32),
        grid_spec=grid_spec,
        compiler_params=_cp(("parallel", "arbitrary")),
        name="sb_decode",
    )(page_table, qm, *([cache_k] * n), *([cache_v] * n))


def _fox_decode_kernel(pt_ref, qm_ref, kn_ref, vn_ref, lfn_ref, *rest):
    del pt_ref
    n = PAGES_PER_STEP
    k_refs, v_refs, lf_refs = rest[:n], rest[n:2 * n], rest[2 * n:3 * n]
    o_ref, acc, m_scr, l_scr, carry = rest[3 * n:]
    g = pl.program_id(1)
    qm = qm_ref[...]

    @pl.when(g == 0)
    def _():
        m_scr[...] = jnp.sum(qm.astype(F32) * kn_ref[...], axis=-1, keepdims=True)
        l_scr[...] = jnp.ones_like(l_scr)
        acc[...] = jnp.broadcast_to(vn_ref[...], acc.shape)
        carry[...] = lfn_ref[...]

    for s in range(n):
        lf = lf_refs[s][...]
        suf = _suffix_sum(lf)
        sc = _dot_nt(qm, k_refs[s][...].astype(BF16)) + (suf - lf) + carry[...]
        m_new = jnp.maximum(m_scr[...], jnp.max(sc, axis=-1, keepdims=True))
        alpha = jnp.exp(m_scr[...] - m_new)
        pr = jnp.exp(sc - m_new)
        l_scr[...] = alpha * l_scr[...] + jnp.sum(pr, axis=-1, keepdims=True)
        acc[...] = alpha * acc[...] + _dot(pr.astype(BF16), v_refs[s][...].astype(BF16))
        m_scr[...] = m_new
        carry[...] += suf[:, 0:1]

    @pl.when(g == pl.num_programs(1) - 1)
    def _():
        o_ref[...] = _head_diag(acc[...] / l_scr[...])


def fox_decode(qm, k_new, v_new, lf_new, cache_k, cache_v, cache_lf_t, page_table):
    b, n_pages = page_table.shape
    n = PAGES_PER_STEP
    lf_specs = [pl.BlockSpec((None, N_HEADS, PAGE),
                             lambda bb, g, pt, s=s: (pt[bb, n_pages - 1 - (g * n + s)], 0, 0))
                for s in range(n)]
    row = pl.BlockSpec((None, 1, HW), lambda bb, g, pt: (bb, 0, 0))
    grid_spec = pltpu.PrefetchScalarGridSpec(
        num_scalar_prefetch=1,
        grid=(b, n_pages // n),
        in_specs=[pl.BlockSpec((None, N_HEADS, HW), lambda bb, g, pt: (bb, 0, 0)), row, row,
                  pl.BlockSpec((None, N_HEADS, 1), lambda bb, g, pt: (bb, 0, 0))]
                 + _page_specs(n_pages) + _page_specs(n_pages) + lf_specs,
        out_specs=row,
        scratch_shapes=[pltpu.VMEM((N_HEADS, HW), F32), pltpu.VMEM((N_HEADS, 1), F32),
                        pltpu.VMEM((N_HEADS, 1), F32), pltpu.VMEM((N_HEADS, 1), F32)],
    )
    return pl.pallas_call(
        _fox_decode_kernel,
        out_shape=jax.ShapeDtypeStruct((b, 1, HW), F32),
        grid_spec=grid_spec,
        compiler_params=_cp(("parallel", "arbitrary")),
        name="fox_decode",
    )(page_table, qm, k_new, v_new, lf_new, *([cache_k] * n), *([cache_v] * n), *([cache_lf_t] * n))


def _expand_heads():
    return jnp.where(_iota2((LANE, HW), 0) == _iota2((LANE, HW), 1) // HEAD_DIM, 1.0, 0.0).astype(BF16)


def _group_rmsnorm(y, gain):
    gw = HW // SSM_GROUPS
    parts = []
    for g in range(SSM_GROUPS):
        yg = y[:, g * gw:(g + 1) * gw]
        parts.append(yg * lax.rsqrt(jnp.mean(yg * yg, axis=-1, keepdims=True) + NORM_EPS))
    return jnp.concatenate(parts, axis=1) * gain


def _mamba_seq_kernel(z_ref, xbc_ref, dt_ref, dtt_ref, cw_ref, cb_ref, dtb_ref, dtbc_ref, alog_ref, alogc_ref,
                      dexp_ref, ng_ref, o_ref, hfin_ref, ext, h_scr):
    n = TILE
    ah_row, ah_col = -jnp.exp(alog_ref[...]), -jnp.exp(alogc_ref[...])

    @pl.when(pl.program_id(0) == 0)
    def _():
        ext[0:8, :] = jnp.zeros((8, ext.shape[1]), F32)
        h_scr[...] = jnp.zeros_like(h_scr)

    ext[8:8 + n, :] = xbc_ref[...]
    conv = cb_ref[...] + ext[5:5 + n, :] * cw_ref[0:1, :]
    for j in range(1, CONV_W):
        conv = conv + ext[5 + j:5 + j + n, :] * cw_ref[j:j + 1, :]
    ext[0:8, :] = ext[n:n + 8, :]
    act = conv * _sigmoid(conv)
    xs = act[:, :HW]
    gs = SSM_STATE
    dt = _softplus(dt_ref[...] + dtb_ref[...])
    a = dt * ah_row
    a_t = _softplus(dtt_ref[...] + dtbc_ref[...]) * ah_col
    row, col = _iota2((n, n), 0), _iota2((n, n), 1)
    lower = jnp.where(row >= col, 1.0, 0.0).astype(BF16)
    upper = jnp.where(row <= col, 1.0, 0.0).astype(BF16)
    a_cum = _dot_01x(lower, a)
    a_cum_t = _dot_x01(a_t, upper)
    ex = _expand_heads()
    a_cum_e = _dot_x01(a_cum, ex)
    xdt = xs * _dot_x01(dt, ex)
    a_last_e = a_cum_e[n - 1:n, :]
    x_end = xdt * jnp.exp(a_last_e - a_cum_e)
    e_cum = jnp.exp(a_cum_e)
    causal = row >= col
    ys = []
    for h in range(N_HEADS):
        g = h // (N_HEADS // SSM_GROUPS)
        sl = slice(h * HEAD_DIM, (h + 1) * HEAD_DIM)
        bm = act[:, HW + g * gs:HW + (g + 1) * gs].astype(BF16)
        cm = act[:, HW + SSM_GROUPS * gs + g * gs:HW + SSM_GROUPS * gs + (g + 1) * gs].astype(BF16)
        seg = a_cum[:, h:h + 1] - a_cum_t[h:h + 1, :]
        decay = jnp.where(causal, jnp.exp(jnp.where(causal, seg, 0.0)), 0.0)
        scores = _dot_nt(cm, bm) * decay
        hs = h_scr[h]
        y = _dot(scores.astype(BF16), xdt[:, sl].astype(BF16))
        y = y + _dot_nt(cm, hs.astype(BF16)) * e_cum[:, sl]
        ys.append(y)
        s_chunk = lax.dot_general(x_end[:, sl].astype(BF16), bm, (((0,), (0,)), ((), ())),
                                  preferred_element_type=F32)
        h_new = hs * jnp.exp(a_cum_t[h:h + 1, n - 1:n]) + s_chunk
        h_scr[h] = h_new
        hfin_ref[h] = h_new
    y = jnp.concatenate(ys, axis=1) + dexp_ref[...] * xs
    zz = z_ref[...]
    y = y * (zz * _sigmoid(zz))
    o_ref[...] = _group_rmsnorm(y, ng_ref[...]).astype(o_ref.dtype)


MAMBA_SEQ_PARAMS = ("conv_w", "conv_b", "dtb_row", "dtb_col", "alog_row", "alog_col", "d_exp", "norm_g")


def mamba_seq(p, dt_t, prm):
    t = p.shape[0]
    full = lambda a: pl.BlockSpec(a.shape, lambda i: (0,) * a.ndim)
    params = [prm[n] for n in MAMBA_SEQ_PARAMS]
    return pl.pallas_call(
        _mamba_seq_kernel,
        out_shape=[jax.ShapeDtypeStruct((t, HW), BF16),
                   jax.ShapeDtypeStruct((N_HEADS, HEAD_DIM, SSM_STATE), F32)],
        grid=(t // TILE,),
        in_specs=[pl.BlockSpec((TILE, HW), lambda i: (i, OD_Z // HW)),
                  pl.BlockSpec((TILE, 2 * HW), lambda i: (i, OD_XBC // (2 * HW))),
                  pl.BlockSpec((TILE, LANE), lambda i: (i, OD_DT // LANE)),
                  pl.BlockSpec((N_HEADS, TILE), lambda i: (0, i))] + [full(a) for a in params],
        out_specs=[pl.BlockSpec((TILE, HW), lambda i: (i, 0)),
                   pl.BlockSpec((N_HEADS, HEAD_DIM, SSM_STATE), lambda i: (0, 0, 0))],
        scratch_shapes=[pltpu.VMEM((TILE + 8, 2 * HW), F32),
                        pltpu.VMEM((N_HEADS, HEAD_DIM, SSM_STATE), F32)],
        compiler_params=_cp(("arbitrary",)),
        name="mamba_seq",
    )(p, p, p, dt_t, *params)


def _mamba_step_kernel(h_ref, cp_ref, z_ref, xbc_ref, dt_ref, cw_ref, cb_ref, dtb_ref, alog_ref, dexp_ref,
                       ng_ref, o_ref, ho_ref):
    nb = h_ref.shape[0]
    ex = _expand_heads()
    ah = -jnp.exp(alog_ref[...])
    gs = SSM_STATE
    for bi in range(nb):
        conv = cb_ref[...] + xbc_ref[bi:bi + 1, :] * cw_ref[CONV_W - 1:CONV_W, :]
        for j in range(CONV_W - 1):
            conv = conv + cp_ref[bi, j:j + 1, :] * cw_ref[j:j + 1, :]
        act = conv * _sigmoid(conv)
        xs = act[:, :HW]
        dt = _softplus(dt_ref[bi:bi + 1, :] + dtb_ref[...])
        da = jnp.exp(dt * ah)
        dd = jnp.concatenate([dt, da, jnp.zeros((6, LANE), F32)], axis=0)
        dde = _dot_x01(dd, ex)
        xdt = xs * dde[0:1, :]
        ys = []
        for h in range(N_HEADS):
            g = h // (N_HEADS // SSM_GROUPS)
            sl = slice(h * HEAD_DIM, (h + 1) * HEAD_DIM)
            bm = act[:, HW + g * gs:HW + (g + 1) * gs]
            cm = act[:, HW + (SSM_GROUPS + g) * gs:HW + (SSM_GROUPS + g + 1) * gs]
            cols = _col_bcast(jnp.concatenate([xdt[:, sl], dde[1:2, sl]], axis=0), gs)
            h_new = h_ref[bi, h] * cols[HEAD_DIM:] + cols[:HEAD_DIM] * bm
            ho_ref[bi, h] = h_new
            ys.append(_dot_nt(jnp.broadcast_to(cm, (8, gs)), h_new, HIGHEST)[0:1, :])
        y = jnp.concatenate(ys, axis=1) + dexp_ref[...] * xs
        zz = z_ref[bi:bi + 1, :]
        y = y * (zz * _sigmoid(zz))
        o_ref[bi:bi + 1, :] = _group_rmsnorm(y, ng_ref[...])


MAMBA_STEP_PARAMS = ("conv_w", "conv_b", "dtb_row", "alog_row", "d_exp", "norm_g")


def mamba_step(h0, conv_prev, p, prm, nb=8):
    b = h0.shape[0]
    full = lambda a: pl.BlockSpec(a.shape, lambda i: (0,) * a.ndim)
    params = [prm[n] for n in MAMBA_STEP_PARAMS]
    st = pl.BlockSpec((nb, N_HEADS, HEAD_DIM, SSM_STATE), lambda i: (i, 0, 0, 0))
    return pl.pallas_call(
        _mamba_step_kernel,
        out_shape=[jax.ShapeDtypeStruct((b, HW), F32), jax.ShapeDtypeStruct(h0.shape, F32)],
        grid=(b // nb,),
        in_specs=[st, pl.BlockSpec((nb, CONV_W - 1, 2 * HW), lambda i: (i, 0, 0)),
                  pl.BlockSpec((nb, HW), lambda i: (i, OD_Z // HW)),
                  pl.BlockSpec((nb, 2 * HW), lambda i: (i, OD_XBC // (2 * HW))),
                  pl.BlockSpec((nb, LANE), lambda i: (i, OD_DT // LANE))] + [full(a) for a in params],
        out_specs=[pl.BlockSpec((nb, HW), lambda i: (i, 0)), st],
        compiler_params=_cp(("parallel",)),
        name="mamba_step",
    )(h0, conv_prev, p, p, p, *params)


LORA_R = 96
RW_PROJ = 3 * HW + 2 * LORA_R + LORA_GATE
N_DT = N_HEADS


def _pad_cols(a, width):
    return jnp.pad(a, [(0, 0)] * (a.ndim - 1) + [(0, width - a.shape[-1])])


def _pad_rows(a, rows):
    return jnp.pad(a, [(0, rows - a.shape[0])] + [(0, 0)] * (a.ndim - 1))


def _rw_split(a):
    o = 3 * HW
    lora = jnp.concatenate([_pad_cols(a[..., o:o + LORA_R], LORA_PAD),
                            _pad_cols(a[..., o + LORA_R:o + 2 * LORA_R], LORA_PAD),
                            a[..., o + 2 * LORA_R:RW_PROJ]], axis=-1)
    return a[..., :o], lora


def _rw_unsplit(p):
    o = EV_LORA
    return jnp.concatenate([p[:, :3 * HW], p[:, o:o + LORA_R], p[:, o + LORA_PAD:o + LORA_PAD + LORA_R],
                            p[:, o + 2 * LORA_PAD:o + LORA_W]], axis=1)


def _even_in_weight(w):
    main, lora = _rw_split(w[:, :RW_PROJ])
    return jnp.concatenate([main, w[:, RW_PROJ:], lora], axis=1).astype(BF16)


def _odd_in_weight(w):
    ssm_proj = HW + 2 * HW + N_DT
    z, xbc, dt = w[:, :HW], w[:, HW:3 * HW], w[:, 3 * HW:ssm_proj]
    qkv, f = w[:, ssm_proj:ssm_proj + 3 * HW], w[:, ssm_proj + 3 * HW:]
    out = jnp.concatenate([xbc, z, qkv, _pad_cols(dt, LANE), _pad_cols(f, LANE)], axis=1)
    return _pad_cols(out, PROJ_PAD).astype(BF16)


def _block_diag_query(q):
    head = jnp.arange(HW, dtype=jnp.int32) // HEAD_DIM
    keep = head[None, None, :] == jnp.arange(N_HEADS, dtype=jnp.int32)[None, :, None]
    return jnp.where(keep, q[:, None, :], jnp.zeros((), q.dtype))


def _row(a):
    return a.reshape(1, -1)


def _heads(a):
    return a.reshape(a.shape[0], N_HEADS, HEAD_DIM)


def _even_layer(x, mod, w, state, page_table):
    sh1, sc1, gt1, sh2, sc2, gt2 = mod
    p = in_proj(x, w["g_mix"], sh1, sc1, w["w_in"])
    if state is None:
        prep = rwkv_prep(p, jnp.zeros((1, RW_MAIN), F32), jnp.zeros((1, LORA_W), F32), w, True)
        o_rw, hfin = rwkv_scan(*prep, w["ln_g"], w["ln_b"])
        s_new = jnp.swapaxes(hfin, 1, 2)[None]
        shift_new = _rw_unsplit(p[-1:])
    else:
        sm, sl = _rw_split(state["shift"])
        prep = rwkv_prep(p, sm, sl, w, False)
        o_rw, s_new = rwkv_step(state["rwkv"], *prep, w["ln_g"], w["ln_b"])
        o_rw = o_rw.astype(BF16)
        shift_new = _rw_unsplit(p)
    qs, kn = qk_norm(p, EV_SBQ, w["qn"], w["kn"])
    vv = p[:, EV_SBV:EV_SBV + HW]
    if state is None:
        o_sb = sb_attention(qs, kn, vv)
    else:
        o_sb = sb_decode(_block_diag_query(qs), state["sb_k"], state["sb_v"], page_table)[:, 0].astype(BF16)
    x = out_proj(o_rw, o_sb, w["wo_a"], w["wo_b"], x, gt1)
    x = ffn(x, w["g_ffn"], sh2, sc2, gt2, w["wg"], w["wu"], w["wd"])
    return x, dict(k=kn, v=vv, rwkv=s_new, shift=shift_new)


def _odd_layer(x, mod, w, state, page_table):
    sh1, sc1, gt1, sh2, sc2, gt2 = mod
    p = in_proj(x, w["g_mix"], sh1, sc1, w["w_in"])
    m = p.shape[0]
    if state is None:
        o_ssm, ssm_new = mamba_seq(p, p[:, OD_DT:OD_DT + N_DT].T, w)
        conv_new = p[m - (CONV_W - 1):, OD_XBC:OD_XBC + 2 * HW]
    else:
        o_ssm, ssm_new = mamba_step(state["ssm"], state["conv"], p, w)
        o_ssm = o_ssm.astype(BF16)
        conv_new = jnp.concatenate([state["conv"][:, 1:], p[:, None, OD_XBC:OD_XBC + 2 * HW]], axis=1)
    qs, kn = qk_norm(p, OD_Q, w["qn"], w["kn"])
    vv = p[:, OD_V:OD_V + HW]
    lf, fc = logf_cumsum(p, OD_F, w["bf"], tt=min(m, 128))
    lf = lf[:, :N_HEADS]
    if state is None:
        fct = fc[:, :N_HEADS].T
        o_fox = fox_attention(qs, kn, vv, fct[:, :, None], fct.reshape(N_HEADS, m // FOX_TK, FOX_TK),
                              w["qn"], w["kn"])
    else:
        o_fox = fox_decode(_block_diag_query(qs), kn[:, None, :], vv[:, None, :], lf[:, :, None],
                           state["fox_k"], state["fox_v"], state["fox_lf_t"], page_table)[:, 0].astype(BF16)
    x = out_proj(o_ssm, o_fox, w["wo_a"], w["wo_b"], x, gt1)
    x = ffn(x, w["g_ffn"], sh2, sc2, gt2, w["wg"], w["wu"], w["wd"])
    return x, dict(k=kn, v=vv, logf=lf, ssm=ssm_new, conv=conv_new)


def kernel(x_prompt, x_sample, cache_sb_k, cache_sb_v, state_rwkv, state_rwkv_shift, cache_fox_k, cache_fox_v, cache_fox_logf, state_ssm, state_conv, page_table, c_prompt, c_sample, w_ada, b_ada, g_mix, g_ffn, w_out, w_ffn_gate, w_ffn_up, w_ffn_down, w_in_even, rw_mu, rw_w0, rw_w2, rw_a0, rw_a2, rw_g2, rw_kk, rw_ka, rw_rk, rw_ln_g, rw_ln_b, sb_qn, sb_kn, w_in_odd, m_conv_w, m_conv_b, m_dt_bias, m_a_log, m_d, m_norm_g, fox_bf, fox_qn, fox_kn):
    nb, t, d = x_prompt.shape
    bs = x_sample.shape[0]
    assert nb == 1 and x_sample.shape[1] == 1 and w_ada.shape[0] == 2
    n_pool = cache_sb_k.shape[1]

    rows = 8 * ((1 + bs + 7) // 8)
    c_all = jnp.concatenate([c_prompt, c_sample, jnp.zeros((rows - 1 - bs, d), F32)], axis=0)
    mod = ada_mod(c_all, w_ada, b_ada)
    mod_p = lambda l: [mod[l, 0:1, i * d:(i + 1) * d] for i in range(6)]
    mod_s = lambda l: [mod[l, 1:1 + bs, i * d:(i + 1) * d] for i in range(6)]

    def common(l):
        return dict(g_mix=_row(g_mix[l]), g_ffn=_row(g_ffn[l]),
                    wo_a=w_out[l, :HW].astype(BF16), wo_b=w_out[l, HW:].astype(BF16),
                    wg=w_ffn_gate[l].astype(BF16), wu=w_ffn_up[l].astype(BF16), wd=w_ffn_down[l].astype(BF16))

    mu_main, mu_lora = _rw_split(_row(rw_mu[0]))
    tile2 = lambda g: _row(jnp.concatenate([g, g]))
    w_even = dict(common(0), w_in=_even_in_weight(w_in_even[0]), mu_main=mu_main, mu_lora=mu_lora,
                  w0=_row(rw_w0[0]), w2=_pad_rows(rw_w2[0], LORA_PAD).astype(BF16),
                  a0=_row(rw_a0[0]), a2=_pad_rows(rw_a2[0], LORA_PAD).astype(BF16),
                  g2=rw_g2[0].astype(BF16), kk=_row(rw_kk[0]), ka=_row(rw_ka[0]), rk=_row(rw_rk[0]),
                  ln_g=_row(rw_ln_g[0]), ln_b=_row(rw_ln_b[0]), qn=tile2(sb_qn[0]), kn=tile2(sb_kn[0]))
    w_odd = dict(common(1), w_in=_odd_in_weight(w_in_odd[0]), conv_w=m_conv_w[0], conv_b=_row(m_conv_b[0]),
                 dtb_row=_pad_cols(_row(m_dt_bias[0]), LANE), dtb_col=m_dt_bias[0][:, None],
                 alog_row=_pad_cols(_row(m_a_log[0]), LANE), alog_col=m_a_log[0][:, None],
                 d_exp=_row(jnp.repeat(m_d[0], HEAD_DIM)), norm_g=_row(m_norm_g[0]),
                 bf=_pad_cols(_row(fox_bf[0]), LANE), qn=tile2(fox_qn[0]), kn=tile2(fox_kn[0]))

    st_even = dict(shift=state_rwkv_shift[0], rwkv=state_rwkv[0],
                   sb_k=cache_sb_k[0].reshape(n_pool, PAGE, HW), sb_v=cache_sb_v[0].reshape(n_pool, PAGE, HW))
    st_odd = dict(ssm=state_ssm[0], conv=state_conv[0],
                  fox_k=cache_fox_k[0].reshape(n_pool, PAGE, HW), fox_v=cache_fox_v[0].reshape(n_pool, PAGE, HW),
                  fox_lf_t=jnp.swapaxes(cache_fox_logf[0], 1, 2))

    xp, ep = _even_layer(x_prompt[0], mod_p(0), w_even, None, None)
    xp, op = _odd_layer(xp, mod_p(1), w_odd, None, None)
    xs, es = _even_layer(x_sample[:, 0], mod_s(0), w_even, st_even, page_table)
    xs, os_ = _odd_layer(xs, mod_s(1), w_odd, st_odd, page_table)

    seq = lambda a: a.reshape((1, 1, t) + a.shape[1:])
    tok = lambda a: a.reshape((1, bs, 1) + a.shape[1:])
    return (xp[None], xs[:, None, :],
            seq(_heads(ep["k"])), tok(_heads(es["k"])), seq(_heads(ep["v"])), tok(_heads(es["v"])),
            ep["rwkv"][None], es["rwkv"][None], ep["shift"][None], es["shift"][None],
            seq(_heads(op["k"])), tok(_heads(os_["k"])), seq(_heads(op["v"])), tok(_heads(os_["v"])),
            seq(op["logf"]), tok(os_["logf"]),
            op["ssm"][None, None], os_["ssm"][None], op["conv"][None, None], os_["conv"][None])
```
